```python
import math
import jax, jax.numpy as jnp
from jax import lax
import numpy as np

D_MODEL = 1024
BATCH = 16
SEQ = 256
DEPTH = 1
DEC_BATCH = 4
DEC_SEQ = 2048
PAST_LEN = 256

GRID_W = 64
N_HEADS = 8
N_KV_HEADS = 2
HEAD_DIM = 128
GROUP = N_HEADS // N_KV_HEADS
ATTN_WIDTH = N_HEADS * HEAD_DIM
KV_WIDTH = N_KV_HEADS * HEAD_DIM
WINDOW = 128
BLOCK = 128
ATTN_SCALE = HEAD_DIM ** -0.5
ROPE_THETA = 10000.0
HYENA_WIDTH = D_MODEL // 2
SHORT_CONV = 3
FILTER_BANDS = 8
FILTER_FEAT = 1 + 2 * FILTER_BANDS
FILTER_HIDDEN = 64
DECAY_TARGET = 1e-2
FAST_DECAY_PCT = 0.3
SLOW_DECAY_PCT = 1.5
D_FF = -(-8 * D_MODEL // (3 * 256)) * 256
IN_WIDTH = ATTN_WIDTH + 2 * KV_WIDTH + 3 * HYENA_WIDTH
EPS = 1e-6
NEG_INF = -1e30

kernel_name = "hybrid_swa_hyena_prefix_diffusion_step"


def rmsnorm(x, g):
    xf = x.astype(jnp.float32)
    y = xf * lax.rsqrt(jnp.mean(xf * xf, axis=-1, keepdims=True) + EPS)
    return (y * g.astype(jnp.float32)).astype(x.dtype)


def modulation(cond, w_mod, b_mod):
    m = jax.nn.silu(cond) @ w_mod + b_mod
    return jnp.split(m[:, None, :], 6, axis=-1)


def axial_rope(x):
    L = x.shape[1]
    rows = L // GRID_W
    row_ids = jnp.repeat(jnp.arange(rows), GRID_W)
    col_ids = jnp.tile(jnp.arange(GRID_W), rows)
    half = HEAD_DIM // 2
    inv_freq = ROPE_THETA ** (-jnp.arange(0, half, 2, dtype=jnp.float32) / half)

    def rot(xa, pos):
        ang = pos.astype(jnp.float32)[:, None] * inv_freq[None, :]
        cos = jnp.cos(ang)[None, :, None, :]
        sin = jnp.sin(ang)[None, :, None, :]
        x1, x2 = jnp.split(xa.astype(jnp.float32), 2, axis=-1)
        return jnp.concatenate([x1 * cos - x2 * sin, x1 * sin + x2 * cos], axis=-1)

    xr, xc = jnp.split(x, 2, axis=-1)
    return jnp.concatenate([rot(xr, row_ids), rot(xc, col_ids)], axis=-1).astype(x.dtype)


def context_attention(q, k, v, sink):
    B, L = q.shape[:2]
    qg = q.reshape(B, L, N_KV_HEADS, GROUP, HEAD_DIM)
    s = jnp.einsum('bqkgd,bckd->bkgqc', qg, k).astype(jnp.float32) * ATTN_SCALE
    sink_b = jnp.broadcast_to(sink.astype(jnp.float32).reshape(1, N_KV_HEADS, GROUP, 1, 1), s.shape[:-1] + (1,))
    p = jax.nn.softmax(jnp.concatenate([s, sink_b], axis=-1), axis=-1)[..., :-1]
    o = jnp.einsum('bkgqc,bckd->bqkgd', p.astype(v.dtype), v)
    return o.reshape(B, L, ATTN_WIDTH)


def latent_attention(q, k, v, ctx_k, ctx_v, sink):
    B, L = q.shape[:2]
    nb = L // BLOCK
    qb = q.reshape(B, nb, BLOCK, N_KV_HEADS, GROUP, HEAD_DIM)
    pad = ((0, 0), (BLOCK, BLOCK), (0, 0), (0, 0))
    kp = jnp.pad(k, pad).reshape(B, nb + 2, BLOCK, N_KV_HEADS, HEAD_DIM)
    vp = jnp.pad(v, pad).reshape(B, nb + 2, BLOCK, N_KV_HEADS, HEAD_DIM)
    kwin = jnp.concatenate([kp[:, :-2], kp[:, 1:-1], kp[:, 2:]], axis=2)
    vwin = jnp.concatenate([vp[:, :-2], vp[:, 1:-1], vp[:, 2:]], axis=2)
    blk = jnp.arange(nb)[:, None]
    qpos = blk * BLOCK + jnp.arange(BLOCK)[None, :]
    kpos = (blk - 1) * BLOCK + jnp.arange(3 * BLOCK)[None, :]
    rel = kpos[:, None, :] - qpos[:, :, None]
    valid = (jnp.abs(rel) <= WINDOW) & (kpos[:, None, :] >= 0) & (kpos[:, None, :] < L)
    s_win = jnp.einsum('bnqkgd,bnskd->bnkgqs', qb, kwin).astype(jnp.float32) * ATTN_SCALE
    s_win = jnp.where(valid[None, :, None, None], s_win, NEG_INF)
    s_ctx = jnp.einsum('bnqkgd,bckd->bnkgqc', qb, ctx_k).astype(jnp.float32) * ATTN_SCALE
    sink_b = jnp.broadcast_to(sink.astype(jnp.float32).reshape(1, 1, N_KV_HEADS, GROUP, 1, 1), s_win.shape[:-1] + (1,))
    p = jax.nn.softmax(jnp.concatenate([s_win, s_ctx, sink_b], axis=-1), axis=-1)
    p_win = p[..., :3 * BLOCK].astype(v.dtype)
    p_ctx = p[..., 3 * BLOCK:-1].astype(v.dtype)
    o = (jnp.einsum('bnkgqs,bnskd->bnqkgd', p_win, vwin)
         + jnp.einsum('bnkgqc,bckd->bnqkgd', p_ctx, ctx_v))
    return o.reshape(B, L, ATTN_WIDTH)


def hyena_filter_spectrum(L, w1, b1, fr1, w2, b2, fr2, w3, b3, decay):
    f32 = jnp.float32
    t = jnp.arange(L, dtype=f32)
    t_norm = t / L
    bands = jnp.linspace(1e-4, FILTER_BANDS - 1, FILTER_BANDS).astype(f32)
    ang = (2.0 * math.pi * t / L)[:, None] * bands[None, :]
    feat = jnp.concatenate([t_norm[:, None], jnp.cos(ang), jnp.sin(ang)], axis=-1)
    h = jnp.sin(fr1.astype(f32) * (feat @ w1.astype(f32) + b1.astype(f32)))
    h = jnp.sin(fr2.astype(f32) * (h @ w2.astype(f32) + b2.astype(f32)))
    h = h @ w3.astype(f32) + b3.astype(f32)
    h = h * jnp.exp(-t_norm[:, None] * jnp.abs(decay.astype(f32))[None, :])
    fwd, bwd = jnp.split(h, 2, axis=-1)
    filt = jnp.concatenate([fwd, jnp.zeros((1, HYENA_WIDTH), f32), bwd[:0:-1]], axis=0)
    return jnp.fft.rfft(filt, axis=0)


def short_conv(u, w, b):
    up = jnp.pad(u, ((0, 0), (1, 1), (0, 0)))
    return up[:, :-2] * w[0] + up[:, 1:-1] * w[1] + up[:, 2:] * w[2] + b


def hyena(u, conv_w, conv_b, spectrum, skip):
    L = u.shape[1]
    u = short_conv(u, conv_w, conv_b)
    x0, x1, v = jnp.split(u, 3, axis=-1)
    z = (v * x1).astype(jnp.float32)
    zf = jnp.fft.rfft(z, n=2 * L, axis=1)
    y = jnp.fft.irfft(zf * spectrum[None], n=2 * L, axis=1)[:, :L] + z * skip.astype(jnp.float32)
    return (x0.astype(jnp.float32) * y).astype(u.dtype)


def mixer(h, p, latent, ctx_k, ctx_v):
    B, L = h.shape[:2]
    proj = h @ p['w_in']
    q, k, v, hy = jnp.split(proj, [ATTN_WIDTH, ATTN_WIDTH + KV_WIDTH, ATTN_WIDTH + 2 * KV_WIDTH], axis=-1)
    q = q.reshape(B, L, N_HEADS, HEAD_DIM)
    k = k.reshape(B, L, N_KV_HEADS, HEAD_DIM)
    v = v.reshape(B, L, N_KV_HEADS, HEAD_DIM)
    if latent:
        attn = latent_attention(axial_rope(q), axial_rope(k), v, ctx_k, ctx_v, p['attn_sink'])
    else:
        attn = context_attention(q, k, v, p['attn_sink'])
    spectrum = hyena_filter_spectrum(L, p['filt_w1'], p['filt_b1'], p['filt_freq1'], p['filt_w2'], p['filt_b2'],
                                     p['filt_freq2'], p['filt_w3'], p['filt_b3'], p['filt_decay'])
    hy_out = hyena(hy, p['conv_w'], p['conv_b'], spectrum, p['hyena_skip'])
    gates = jax.nn.sigmoid((h @ p['w_gate'] + p['b_gate']).astype(jnp.float32)).astype(h.dtype)
    g_a, g_h = jnp.split(gates, 2, axis=-1)
    merged = g_a * (attn @ p['w_pa']) + g_h * (hy_out @ p['w_ph'])
    return merged @ p['w_o'], k, v


def layer(x, cond, p, latent, ctx_k, ctx_v):
    sh1, sc1, g1, sh2, sc2, g2 = modulation(cond, p['w_mod'], p['b_mod'])
    h = rmsnorm(x, p['norm_mix_pre']) * (1 + sc1) + sh1
    mix, k, v = mixer(h, p, latent, ctx_k, ctx_v)
    x = x + g1 * rmsnorm(mix, p['norm_mix_post'])
    h = rmsnorm(x, p['norm_ffn_pre']) * (1 + sc2) + sh2
    gt, up = jnp.split(h @ p['w_up'], 2, axis=-1)
    f = (jax.nn.silu(gt) * up) @ p['w_down']
    x = x + g2 * rmsnorm(f, p['norm_ffn_post'])
    return x, k, v


def setup_inputs(seed: int = 0) -> dict:
    key = jax.random.key(seed)
    ks = iter(jax.random.split(key, 40))
    nrm = lambda shape, s=1.0: jax.random.normal(next(ks), shape, jnp.float32) * s
    D = D_MODEL
    max_decay = abs(math.log(DECAY_TARGET)) / FAST_DECAY_PCT
    min_decay = abs(math.log(DECAY_TARGET)) / SLOW_DECAY_PCT
    return {
        'x_prompt': nrm((BATCH, SEQ, D)),
        'x_sample': nrm((DEC_BATCH, DEC_SEQ, D)),
        'c': nrm((DEC_BATCH, D)),
        'cache_k': nrm((DEC_BATCH, DEPTH, PAST_LEN, N_KV_HEADS, HEAD_DIM)),
        'cache_v': nrm((DEC_BATCH, DEPTH, PAST_LEN, N_KV_HEADS, HEAD_DIM)),
        'c_ctx': nrm((D,)),
        'norm_mix_pre': 1.0 + nrm((DEPTH, D), 0.1),
        'norm_mix_post': 1.0 + nrm((DEPTH, D), 0.1),
        'norm_ffn_pre': 1.0 + nrm((DEPTH, D), 0.1),
        'norm_ffn_post': 1.0 + nrm((DEPTH, D), 0.1),
        'w_mod': nrm((DEPTH, D, 6 * D), 0.5 * D ** -0.5),
        'b_mod': nrm((DEPTH, 6 * D), 0.01),
        'w_in': nrm((DEPTH, D, IN_WIDTH), D ** -0.5),
        'attn_sink': nrm((DEPTH, N_HEADS)),
        'conv_w': nrm((DEPTH, SHORT_CONV, 3 * HYENA_WIDTH), 0.5),
        'conv_b': nrm((DEPTH, 3 * HYENA_WIDTH), 0.01),
        'filt_w1': nrm((DEPTH, FILTER_FEAT, FILTER_HIDDEN), FILTER_FEAT ** -0.5),
        'filt_b1': nrm((DEPTH, FILTER_HIDDEN), 0.1),
        'filt_freq1': 1.0 + nrm((DEPTH, FILTER_HIDDEN), 0.1),
        'filt_w2': nrm((DEPTH, FILTER_HIDDEN, FILTER_HIDDEN), FILTER_HIDDEN ** -0.5),
        'filt_b2': nrm((DEPTH, FILTER_HIDDEN), 0.1),
        'filt_freq2': 1.0 + nrm((DEPTH, FILTER_HIDDEN), 0.1),
        'filt_w3': nrm((DEPTH, FILTER_HIDDEN, 2 * HYENA_WIDTH), FILTER_HIDDEN ** -0.5),
        'filt_b3': nrm((DEPTH, 2 * HYENA_WIDTH), 0.01),
        'filt_decay': jax.random.uniform(next(ks), (DEPTH, 2 * HYENA_WIDTH), jnp.float32, min_decay, max_decay),
        'hyena_skip': nrm((DEPTH, HYENA_WIDTH), 0.5),
        'w_pa': nrm((DEPTH, ATTN_WIDTH, D), ATTN_WIDTH ** -0.5),
        'w_ph': nrm((DEPTH, HYENA_WIDTH, D), HYENA_WIDTH ** -0.5),
        'w_gate': nrm((DEPTH, D, 2 * D), D ** -0.5),
        'b_gate': nrm((DEPTH, 2 * D), 0.01),
        'w_o': nrm((DEPTH, D, D), D ** -0.5),
        'w_up': nrm((DEPTH, D, 2 * D_FF), D ** -0.5),
        'w_down': nrm((DEPTH, D_FF, D), D_FF ** -0.5),
    }


def reference(x_prompt, x_sample, c, cache_k, cache_v, c_ctx,
              norm_mix_pre, norm_mix_post, norm_ffn_pre, norm_ffn_post, w_mod, b_mod,
              w_in, attn_sink, conv_w, conv_b, filt_w1, filt_b1, filt_freq1, filt_w2, filt_b2,
              filt_freq2, filt_w3, filt_b3, filt_decay, hyena_skip, w_pa, w_ph, w_gate, b_gate,
              w_o, w_up, w_down):
    y_prompt = x_prompt
    y_sample = x_sample
    new_k_layers = []
    new_v_layers = []
    for l in range(DEPTH):
        p = {
            'norm_mix_pre': norm_mix_pre[l], 'norm_mix_post': norm_mix_post[l],
            'norm_ffn_pre': norm_ffn_pre[l], 'norm_ffn_post': norm_ffn_post[l],
            'w_mod': w_mod[l], 'b_mod': b_mod[l], 'w_in': w_in[l], 'attn_sink': attn_sink[l],
            'conv_w': conv_w[l], 'conv_b': conv_b[l],
            'filt_w1': filt_w1[l], 'filt_b1': filt_b1[l], 'filt_freq1': filt_freq1[l],
            'filt_w2': filt_w2[l], 'filt_b2': filt_b2[l], 'filt_freq2': filt_freq2[l],
            'filt_w3': filt_w3[l], 'filt_b3': filt_b3[l], 'filt_decay': filt_decay[l],
            'hyena_skip': hyena_skip[l], 'w_pa': w_pa[l], 'w_ph': w_ph[l],
            'w_gate': w_gate[l], 'b_gate': b_gate[l], 'w_o': w_o[l],
            'w_up': w_up[l], 'w_down': w_down[l],
        }
        y_prompt, k_ctx, v_ctx = layer(y_prompt, c_ctx[None, :], p, False, None, None)
        new_k_layers.append(k_ctx)
        new_v_layers.append(v_ctx)
        y_sample, _, _ = layer(y_sample, c, p, True, cache_k[:, l], cache_v[:, l])
    new_k = jnp.stack(new_k_layers, axis=1)
    new_v = jnp.stack(new_v_layers, axis=1)
    return (y_prompt, y_sample, new_k, new_v)
```

```python
import functools
import math

import numpy as np
import jax
import jax.numpy as jnp
from jax import lax
from jax.experimental import pallas as pl
from jax.experimental.pallas import tpu as pltpu

D_MODEL = 1024
GRID_W = 64
N_HEADS = 8
N_KV_HEADS = 2
HEAD_DIM = 128
GROUP = N_HEADS // N_KV_HEADS
ATTN_WIDTH = N_HEADS * HEAD_DIM
KV_WIDTH = N_KV_HEADS * HEAD_DIM
WINDOW = 128
ATTN_SCALE = HEAD_DIM ** -0.5
ROPE_THETA = 10000.0
HYENA_WIDTH = D_MODEL // 2
FILTER_BANDS = 8
FILTER_FEAT = 1 + 2 * FILTER_BANDS
FILTER_FEAT_PAD = 32
FILTER_HIDDEN = 64
D_FF = -(-8 * D_MODEL // (3 * 256)) * 256
IN_WIDTH = ATTN_WIDTH + 2 * KV_WIDTH + 3 * HYENA_WIDTH
EPS = 1e-6
NEG_INF = -1e30

V7X_VMEM_BYTES = 64 * 1024 * 1024
VMEM_LIMIT = V7X_VMEM_BYTES - 8 * 1024 * 1024
MXU_COLS = 256
FFN_CHUNK = MXU_COLS
ROW_TILE = 512
ATTN_Q_TILE = 256
ATTN_KEY_SPAN = ATTN_Q_TILE + 2 * WINDOW
HYENA_CH_TILE = 256

BF16 = jnp.bfloat16
F32 = jnp.float32


def _params(*semantics):
    return pltpu.CompilerParams(dimension_semantics=semantics, vmem_limit_bytes=VMEM_LIMIT)


def _const_spec(shape):
    zeros = (0,) * len(shape)
    return pl.BlockSpec(shape, lambda *_: zeros, pipeline_mode=pl.Buffered(1))


def _dot(a, b):
    return jnp.dot(a, b, preferred_element_type=F32)


def _rmsnorm(x, g):
    return x * lax.rsqrt(jnp.mean(x * x, axis=-1, keepdims=True) + EPS) * g


def _silu(x):
    return x * jax.nn.sigmoid(x)


def _mod_kernel(cond_ref, w_ref, b_ref, o_ref):
    a = _silu(cond_ref[...]).astype(BF16)
    o_ref[...] = _dot(a, w_ref[...].astype(BF16)) + b_ref[...]


def _modulation(cond, w_mod, b_mod):
    rows = cond.shape[0]
    n = w_mod.shape[1]
    tn = D_MODEL
    return pl.pallas_call(
        _mod_kernel,
        grid=(n // tn,),
        in_specs=[pl.BlockSpec((rows, D_MODEL), lambda j: (0, 0)),
                  pl.BlockSpec((D_MODEL, tn), lambda j: (0, j)),
                  pl.BlockSpec((1, tn), lambda j: (0, j))],
        out_specs=pl.BlockSpec((rows, tn), lambda j: (0, j)),
        out_shape=jax.ShapeDtypeStruct((rows, n), F32),
        compiler_params=_params("arbitrary"),
        name="modulation",
    )(cond, w_mod, b_mod.reshape(1, n))


def _dft_tables(L):
    bs = 1 << (int(math.log2(L)) // 2)
    nb = L // bs
    t = np.arange(L, dtype=np.int64)
    theta = math.pi / L
    ang_a = ((np.arange(nb, dtype=np.int64)[:, None] * bs * t[None, :]) % (2 * L)) * theta
    ang_b = ((np.arange(bs, dtype=np.int64)[:, None] * t[None, :]) % (2 * L)) * theta
    ca = jnp.asarray(np.cos(ang_a), F32)[:, None, :]
    sa = jnp.asarray(np.sin(ang_a), F32)[:, None, :]
    cb = jnp.asarray(np.cos(ang_b), F32)[None, :, :]
    sb = jnp.asarray(np.sin(ang_b), F32)[None, :, :]
    cos_t = (ca * cb - sa * sb).reshape(L, L).astype(BF16)
    sin_t = (sa * cb + ca * sb).reshape(L, L).astype(BF16)
    return cos_t, sin_t


def _spectrum_kernel(feat_ref, w1_ref, b1_ref, f1_ref, w2_ref, b2_ref, f2_ref, w3_ref, b3_ref,
                     decay_ref, tn_ref, cos_ref, sin_ref, hr_ref, hi_ref, hn_ref, a_s, b_s, *, L, tk):
    i = pl.program_id(0)
    hi_prec = lax.Precision.HIGHEST

    @pl.when(i == 0)
    def _():
        h = jnp.dot(feat_ref[...], w1_ref[...], precision=hi_prec, preferred_element_type=F32)
        h = jnp.sin(f1_ref[...] * (h + b1_ref[...]))
        h = jnp.dot(h, w2_ref[...], precision=hi_prec, preferred_element_type=F32)
        h = jnp.sin(f2_ref[...] * (h + b2_ref[...]))
        h = jnp.dot(h, w3_ref[...], precision=hi_prec, preferred_element_type=F32) + b3_ref[...]
        h = h * jnp.exp(-tn_ref[...] * jnp.abs(decay_ref[...]))
        fwd = h[:, :HYENA_WIDTH]
        bwd = h[:, HYENA_WIDTH:]
        row = lax.broadcasted_iota(jnp.int32, (L, HYENA_WIDTH), 0)
        bwd = jnp.where(row == 0, 0.0, bwd)
        even = fwd + bwd
        a_s[...] = even.astype(BF16)
        b_s[...] = (fwd - bwd).astype(BF16)
        sign = jnp.where((row & 1) == 0, 1.0, -1.0)
        hn_ref[...] = jnp.sum(even * sign, axis=0, keepdims=True) * (1.0 / (2 * L))

    krow = lax.broadcasted_iota(jnp.int32, (tk, HYENA_WIDTH), 0) + i * tk
    wk = jnp.where(krow == 0, 1.0, 2.0) * (1.0 / (2 * L))
    hr_ref[...] = _dot(cos_ref[...], a_s[...]) * wk
    hi_ref[...] = _dot(sin_ref[...], b_s[...]) * (-wk)


def _spectrum(L, feat, tnorm, cos_t, sin_t, w1, b1, f1, w2, b2, f2, w3, b3, decay):
    tk = min(L, 512)
    C = HYENA_WIDTH
    small = [feat, w1, b1, f1, w2, b2, f2, w3, b3, decay, tnorm]
    return pl.pallas_call(
        functools.partial(_spectrum_kernel, L=L, tk=tk),
        grid=(L // tk,),
        in_specs=[_const_spec(a.shape) for a in small]
                 + [pl.BlockSpec((tk, L), lambda i: (i, 0)), pl.BlockSpec((tk, L), lambda i: (i, 0))],
        out_specs=[pl.BlockSpec((tk, C), lambda i: (i, 0)), pl.BlockSpec((tk, C), lambda i: (i, 0)),
                   pl.BlockSpec((1, C), lambda i: (0, 0))],
        out_shape=[jax.ShapeDtypeStruct((L, C), F32), jax.ShapeDtypeStruct((L, C), F32),
                   jax.ShapeDtypeStruct((1, C), F32)],
        scratch_shapes=[pltpu.VMEM((L, C), BF16), pltpu.VMEM((L, C), BF16)],
        compiler_params=_params("arbitrary"),
        name=f"spectrum_{L}",
    )(*small, cos_t, sin_t)


def _modulated_norm(x, g, shift, scale):
    return _rmsnorm(x, g) * (1.0 + scale) + shift


def _rope(x, cos, sin_signed):
    lane = lax.broadcasted_iota(jnp.int32, x.shape, 1)
    swapped = jnp.where((lane & 63) < 32,
                        pltpu.roll(x, HEAD_DIM - 32, axis=1),
                        pltpu.roll(x, 32, axis=1))
    return x * cos + swapped * sin_signed


def _inproj_kernel(*refs, rope):
    if rope:
        x_ref, mod_ref, g_ref, w_ref, cos_ref, sin_ref, q_ref, k_ref, v_ref, hy_ref = refs
    else:
        x_ref, mod_ref, g_ref, w_ref, q_ref, k_ref, v_ref, hy_ref = refs
    m = mod_ref[0]
    h = _modulated_norm(x_ref[...], g_ref[...], m[:, 0:D_MODEL], m[:, D_MODEL:2 * D_MODEL]).astype(BF16)

    def head(col):
        y = _dot(h, w_ref[:, col:col + HEAD_DIM])
        if rope:
            y = _rope(y, cos_ref[...], sin_ref[...])
        return y

    for hd in range(N_HEADS):
        q_ref[:, hd * HEAD_DIM:(hd + 1) * HEAD_DIM] = (head(hd * HEAD_DIM) * ATTN_SCALE).astype(q_ref.dtype)
    for hd in range(N_KV_HEADS):
        k_ref[:, hd * HEAD_DIM:(hd + 1) * HEAD_DIM] = head(ATTN_WIDTH + hd * HEAD_DIM).astype(k_ref.dtype)
    v0 = ATTN_WIDTH + KV_WIDTH
    v_ref[...] = _dot(h, w_ref[:, v0:v0 + KV_WIDTH]).astype(v_ref.dtype)
    hy0 = ATTN_WIDTH + 2 * KV_WIDTH
    hy_ref[...] = _dot(h, w_ref[:, hy0:]).astype(hy_ref.dtype)


def _inproj(x, mod, g_pre, w_in, L, rope_tabs, kv_dtype):
    R = x.shape[0]
    tm = ROW_TILE
    steps_per_mod = (R // mod.shape[0]) // tm
    row = lambda i: (i, 0)
    in_specs = [pl.BlockSpec((tm, D_MODEL), row),
                pl.BlockSpec((1, 1, mod.shape[2]), lambda i: (i // steps_per_mod, 0, 0)),
                _const_spec(g_pre.shape), _const_spec(w_in.shape)]
    args = [x, mod, g_pre, w_in]
    if rope_tabs is not None:
        steps_per_seq = L // tm
        tab = pl.BlockSpec((tm, HEAD_DIM), lambda i: (i % steps_per_seq, 0))
        in_specs += [tab, tab]
        args += list(rope_tabs)
    return pl.pallas_call(
        functools.partial(_inproj_kernel, rope=rope_tabs is not None),
        grid=(R // tm,),
        in_specs=in_specs,
        out_specs=[pl.BlockSpec((tm, ATTN_WIDTH), row), pl.BlockSpec((tm, KV_WIDTH), row),
                   pl.BlockSpec((tm, KV_WIDTH), row), pl.BlockSpec((tm, 3 * HYENA_WIDTH), row)],
        out_shape=[jax.ShapeDtypeStruct((R, ATTN_WIDTH), BF16), jax.ShapeDtypeStruct((R, KV_WIDTH), kv_dtype),
                   jax.ShapeDtypeStruct((R, KV_WIDTH), kv_dtype),
                   jax.ShapeDtypeStruct((R, 3 * HYENA_WIDTH), BF16)],
        compiler_params=_params("arbitrary"),
        name=f"inproj_{L}",
    )(*args)


def _rope_tables(L):
    rows = L // GRID_W
    row_ids = jnp.repeat(jnp.arange(rows), GRID_W)
    col_ids = jnp.tile(jnp.arange(GRID_W), rows)
    half = HEAD_DIM // 2
    inv_freq = ROPE_THETA ** (-jnp.arange(0, half, 2, dtype=F32) / half)
    cos_parts, sin_parts = [], []
    for pos in (row_ids, col_ids):
        ang = pos.astype(F32)[:, None] * inv_freq[None, :]
        cos_parts += [jnp.cos(ang), jnp.cos(ang)]
        sin_parts += [-jnp.sin(ang), jnp.sin(ang)]
    return jnp.concatenate(cos_parts, axis=-1), jnp.concatenate(sin_parts, axis=-1)


def _softmax_pv(s, sink_col, v):
    m = jnp.maximum(jnp.max(s, axis=-1, keepdims=True), sink_col)
    p = jnp.exp(s - m)
    denom = jnp.sum(p, axis=-1, keepdims=True) + jnp.exp(sink_col - m)
    return _dot(p.astype(BF16), v) * (1.0 / denom)


def _attn_kernel(*refs, tq, windowed, L):
    if windowed:
        sink_ref, q_ref, k_ref, v_ref, ck_ref, cv_ref, o_ref = refs
    else:
        sink_ref, q_ref, k_ref, v_ref, o_ref = refs
    if windowed:
        i = pl.program_id(1)
        start = pl.multiple_of(jnp.clip(i * tq - WINDOW, 0, L - ATTN_KEY_SPAN), WINDOW)
        n_ctx = ck_ref.shape[0]
        n_keys = ATTN_KEY_SPAN + n_ctx
        r = lax.broadcasted_iota(jnp.int32, (GROUP * tq, n_keys), 0)
        c = lax.broadcasted_iota(jnp.int32, (GROUP * tq, n_keys), 1)
        rel = (start + c) - (i * tq + (r & (tq - 1)))
        valid = (c >= ATTN_KEY_SPAN) | (jnp.abs(rel) <= WINDOW)
    for kk in range(N_KV_HEADS):
        heads = [kk * GROUP + g for g in range(GROUP)]
        q4 = jnp.concatenate([q_ref[:, hd * HEAD_DIM:(hd + 1) * HEAD_DIM] for hd in heads], axis=0)
        sink_col = jnp.concatenate([jnp.full((tq, 1), sink_ref[hd], F32) for hd in heads], axis=0)
        cols = slice(kk * HEAD_DIM, (kk + 1) * HEAD_DIM)
        if windowed:
            keys = jnp.concatenate([k_ref[pl.ds(start, ATTN_KEY_SPAN), cols],
                                    ck_ref[:, cols].astype(BF16)], axis=0)
            vals = jnp.concatenate([v_ref[pl.ds(start, ATTN_KEY_SPAN), cols],
                                    cv_ref[:, cols].astype(BF16)], axis=0)
        else:
            keys = k_ref[:, cols].astype(BF16)
            vals = v_ref[:, cols].astype(BF16)
        s = lax.dot_general(q4, keys, (((1,), (1,)), ((), ())), preferred_element_type=F32)
        if windowed:
            s = jnp.where(valid, s, NEG_INF)
        o = _softmax_pv(s, sink_col, vals)
        for g, hd in enumerate(heads):
            o_ref[:, hd * HEAD_DIM:(hd + 1) * HEAD_DIM] = o[g * tq:(g + 1) * tq].astype(o_ref.dtype)


def _attention(q, k, v, sink, B, L, ctx_k=None, ctx_v=None):
    windowed = ctx_k is not None
    tq = ATTN_Q_TILE if windowed else L
    nq = L // tq
    sink_spec = pl.BlockSpec(memory_space=pltpu.SMEM)
    q_spec = pl.BlockSpec((tq, ATTN_WIDTH), lambda b, i: (b * nq + i, 0))
    seq_spec = pl.BlockSpec((L, KV_WIDTH), lambda b, i: (b, 0))
    in_specs = [sink_spec, q_spec, seq_spec, seq_spec]
    args = [sink, q, k, v]
    if windowed:
        ctx_spec = pl.BlockSpec((ctx_k.shape[0] // B, KV_WIDTH), lambda b, i: (b, 0))
        in_specs += [ctx_spec, ctx_spec]
        args += [ctx_k, ctx_v]
    return pl.pallas_call(
        functools.partial(_attn_kernel, tq=tq, windowed=windowed, L=L),
        grid=(B, nq),
        in_specs=in_specs,
        out_specs=q_spec,
        out_shape=jax.ShapeDtypeStruct((B * L, ATTN_WIDTH), BF16),
        compiler_params=_params("arbitrary", "arbitrary"),
        name=f"attention_{L}",
    )(*args)


def _hyena_kernel(x0_ref, x1_ref, v_ref, cw0_ref, cw1_ref, cwv_ref, cb0_ref, cb1_ref, cbv_ref,
                  hr_ref, hi_ref, hn_ref, skip_ref, cos_ref, sin_ref, o_ref, *, L):
    tc = o_ref.shape[1]
    row = lax.broadcasted_iota(jnp.int32, (L, tc), 0)
    first = row == 0
    last = row == L - 1

    def short_conv(u_ref, w_ref, b_ref):
        u = u_ref[...].astype(F32)
        prev = jnp.where(first, 0.0, pltpu.roll(u, 1, axis=0))
        nxt = jnp.where(last, 0.0, pltpu.roll(u, L - 1, axis=0))
        return prev * w_ref[0:1, :] + u * w_ref[1:2, :] + nxt * w_ref[2:3, :] + b_ref[...]

    z = short_conv(v_ref, cwv_ref, cbv_ref) * short_conv(x1_ref, cw1_ref, cb1_ref)
    zb = z.astype(BF16)
    sign = jnp.where((row & 1) == 0, 1.0, -1.0)
    z_nyq = jnp.sum(z * sign, axis=0, keepdims=True)
    zc = _dot(cos_ref[...], zb)
    zs = _dot(sin_ref[...], zb)
    hr = hr_ref[...]
    hi = hi_ref[...]
    wr = (zc * hr + zs * hi).astype(BF16)
    wi = (zc * hi - zs * hr).astype(BF16)
    y = _dot(cos_ref[...], wr) - _dot(sin_ref[...], wi)
    y = y + sign * (z_nyq * hn_ref[...]) + z * skip_ref[...]
    o_ref[...] = (short_conv(x0_ref, cw0_ref, cb0_ref) * y).astype(o_ref.dtype)


def _hyena(hy, conv_w, conv_b, hr, hi, hn, skip, cos_t, sin_t, B, L):
    C = HYENA_WIDTH
    tc = HYENA_CH_TILE
    nj = C // tc

    def part(p, rows):
        return pl.BlockSpec((rows, tc), lambda b, j: (b if rows == L else 0, p * nj + j))

    spec_c = pl.BlockSpec((L, tc), lambda b, j: (0, j))
    vec_c = pl.BlockSpec((1, tc), lambda b, j: (0, j))
    conv_b = conv_b.reshape(1, 3 * C)
    return pl.pallas_call(
        functools.partial(_hyena_kernel, L=L),
        grid=(B, nj),
        in_specs=[part(0, L), part(1, L), part(2, L),
                  part(0, 3), part(1, 3), part(2, 3),
                  part(0, 1), part(1, 1), part(2, 1),
                  spec_c, spec_c, vec_c, vec_c,
                  _const_spec((L, L)), _const_spec((L, L))],
        out_specs=pl.BlockSpec((L, tc), lambda b, j: (b, j)),
        out_shape=jax.ShapeDtypeStruct((B * L, C), BF16),
        compiler_params=_params("arbitrary", "arbitrary"),
        name=f"hyena_{L}",
    )(hy, hy, hy, conv_w, conv_w, conv_w, conv_b, conv_b, conv_b,
      hr, hi, hn, skip.reshape(1, C), cos_t, sin_t)


def _merge_ffn_kernel(x_ref, attn_ref, hyo_ref, mod_ref, g_pre_ref, g_post_ref, g_fpre_ref, g_fpost_ref,
                      wg_ref, bg_ref, wpa_ref, wph_ref, wo_ref, wup_ref, wdn_ref, o_ref):
    D = D_MODEL
    m = mod_ref[0]
    sh1, sc1, g1, sh2, sc2, g2 = [m[:, j * D:(j + 1) * D] for j in range(6)]
    x = x_ref[...]
    h = _modulated_norm(x, g_pre_ref[...], sh1, sc1).astype(BF16)
    gate_a = jax.nn.sigmoid(_dot(h, wg_ref[:, :D]) + bg_ref[:, :D])
    merged = gate_a * _dot(attn_ref[...], wpa_ref[...])
    gate_h = jax.nn.sigmoid(_dot(h, wg_ref[:, D:]) + bg_ref[:, D:])
    merged = merged + gate_h * _dot(hyo_ref[...], wph_ref[...])
    mix = _dot(merged.astype(BF16), wo_ref[...])
    x = x + g1 * _rmsnorm(mix, g_post_ref[...])
    h2 = _modulated_norm(x, g_fpre_ref[...], sh2, sc2).astype(BF16)
    f = jnp.zeros(x.shape, F32)
    for c0 in range(0, D_FF, FFN_CHUNK):
        gt = _dot(h2, wup_ref[:, c0:c0 + FFN_CHUNK])
        up = _dot(h2, wup_ref[:, D_FF + c0:D_FF + c0 + FFN_CHUNK])
        f = f + _dot((_silu(gt) * up).astype(BF16), wdn_ref[c0:c0 + FFN_CHUNK, :])
    o_ref[...] = x + g2 * _rmsnorm(f, g_fpost_ref[...])


def _merge_ffn(x, attn, hyo, mod, norms, w_gate, b_gate, w_pa, w_ph, w_o, w_up, w_down, L):
    R = x.shape[0]
    tm = ROW_TILE
    steps_per_mod = (R // mod.shape[0]) // tm
    row = lambda i: (i, 0)
    consts = list(norms) + [w_gate, b_gate, w_pa, w_ph, w_o, w_up, w_down]
    return pl.pallas_call(
        _merge_ffn_kernel,
        grid=(R // tm,),
        in_specs=[pl.BlockSpec((tm, D_MODEL), row), pl.BlockSpec((tm, ATTN_WIDTH), row),
                  pl.BlockSpec((tm, HYENA_WIDTH), row),
                  pl.BlockSpec((1, 1, mod.shape[2]), lambda i: (i // steps_per_mod, 0, 0))]
                 + [_const_spec(a.shape) for a in consts],
        out_specs=pl.BlockSpec((tm, D_MODEL), row),
        out_shape=jax.ShapeDtypeStruct((R, D_MODEL), F32),
        compiler_params=_params("arbitrary"),
        name=f"merge_ffn_{L}",
    )(x, attn, hyo, mod, *consts)


def _filter_features(L):
    t = jnp.arange(L, dtype=F32)
    t_norm = t / L
    bands = jnp.linspace(1e-4, FILTER_BANDS - 1, FILTER_BANDS).astype(F32)
    ang = (2.0 * math.pi * t / L)[:, None] * bands[None, :]
    feat = jnp.concatenate([t_norm[:, None], jnp.cos(ang), jnp.sin(ang)], axis=-1)
    feat = jnp.pad(feat, ((0, 0), (0, FILTER_FEAT_PAD - FILTER_FEAT)))
    return feat, t_norm[:, None]


def _layer(x, mod, p, B, L, latent, ctx_k, ctx_v):
    cos_t, sin_t = _dft_tables(L)
    feat, tnorm = _filter_features(L)
    hr, hi, hn = _spectrum(L, feat, tnorm, cos_t, sin_t, p['filt_w1'], p['filt_b1'], p['filt_freq1'],
                           p['filt_w2'], p['filt_b2'], p['filt_freq2'], p['filt_w3'], p['filt_b3'],
                           p['filt_decay'])
    rope_tabs = _rope_tables(L) if latent else None
    q, k, v, hy = _inproj(x, mod, p['norm_mix_pre'], p['w_in'], L, rope_tabs, BF16 if latent else F32)
    attn = _attention(q, k, v, p['attn_sink'], B, L, ctx_k, ctx_v)
    hyo = _hyena(hy, p['conv_w'], p['conv_b'], hr, hi, hn, p['hyena_skip'], cos_t, sin_t, B, L)
    norms = [p['norm_mix_pre'], p['norm_mix_post'], p['norm_ffn_pre'], p['norm_ffn_post']]
    y = _merge_ffn(x, attn, hyo, mod, norms, p['w_gate'], p['b_gate'], p['w_pa'], p['w_ph'], p['w_o'],
                   p['w_up'], p['w_down'], L)
    return y, k, v


def kernel(x_prompt, x_sample, c, cache_k, cache_v, c_ctx, norm_mix_pre, norm_mix_post, norm_ffn_pre, norm_ffn_post, w_mod, b_mod, w_in, attn_sink, conv_w, conv_b, filt_w1, filt_b1, filt_freq1, filt_w2, filt_b2, filt_freq2, filt_w3, filt_b3, filt_decay, hyena_skip, w_pa, w_ph, w_gate, b_gate, w_o, w_up, w_down):
    Bp, Lp, D = x_prompt.shape
    Bs, Ls, _ = x_sample.shape
    depth = w_in.shape[0]
    past = cache_k.shape[2]
    yp = x_prompt.reshape(Bp * Lp, D)
    ys = x_sample.reshape(Bs * Ls, D)
    cond = jnp.concatenate([c_ctx[None, :], c, jnp.zeros((8 - 1 - Bs, D), F32)], axis=0)
    new_k, new_v = [], []
    row2 = lambda a: a.reshape(1, -1)
    for l in range(depth):
        w1 = jnp.pad(filt_w1[l], ((0, FILTER_FEAT_PAD - FILTER_FEAT), (0, 0)))
        p = {
            'norm_mix_pre': row2(norm_mix_pre[l]), 'norm_mix_post': row2(norm_mix_post[l]),
            'norm_ffn_pre': row2(norm_ffn_pre[l]), 'norm_ffn_post': row2(norm_ffn_post[l]),
            'w_in': w_in[l].astype(BF16), 'attn_sink': attn_sink[l],
            'conv_w': conv_w[l], 'conv_b': conv_b[l],
            'filt_w1': w1, 'filt_b1': row2(filt_b1[l]), 'filt_freq1': row2(filt_freq1[l]),
            'filt_w2': filt_w2[l], 'filt_b2': row2(filt_b2[l]), 'filt_freq2': row2(filt_freq2[l]),
            'filt_w3': filt_w3[l], 'filt_b3': row2(filt_b3[l]), 'filt_decay': row2(filt_decay[l]),
            'hyena_skip': hyena_skip[l], 'w_pa': w_pa[l].astype(BF16), 'w_ph': w_ph[l].astype(BF16),
            'w_gate': w_gate[l].astype(BF16), 'b_gate': row2(b_gate[l]), 'w_o': w_o[l].astype(BF16),
            'w_up': w_up[l].astype(BF16), 'w_down': w_down[l].astype(BF16),
        }
        mod = _modulation(cond, w_mod[l], b_mod[l])
        mod_p = mod[0:1].reshape(1, 1, 6 * D)
        mod_s = mod[1:1 + Bs].reshape(Bs, 1, 6 * D)
        yp, k_ctx, v_ctx = _layer(yp, mod_p, p, Bp, Lp, False, None, None)
        new_k.append(k_ctx.reshape(Bp, Lp, N_KV_HEADS, HEAD_DIM))
        new_v.append(v_ctx.reshape(Bp, Lp, N_KV_HEADS, HEAD_DIM))
        ck = cache_k[:, l].reshape(Bs * past, KV_WIDTH)
        cv = cache_v[:, l].reshape(Bs * past, KV_WIDTH)
        ys, _, _ = _layer(ys, mod_s, p, Bs, Ls, True, ck, cv)
    return (yp.reshape(Bp, Lp, D), ys.reshape(Bs, Ls, D),
            jnp.stack(new_k, axis=1), jnp.stack(new_v, axis=1))
```

```python
import functools
import math

import numpy as np
import jax
import jax.numpy as jnp
from jax import lax
from jax.experimental import pallas as pl
from jax.experimental.pallas import tpu as pltpu

D_MODEL = 1024
GRID_W = 64
N_HEADS = 8
N_KV_HEADS = 2
HEAD_DIM = 128
GROUP = N_HEADS // N_KV_HEADS
ATTN_WIDTH = N_HEADS * HEAD_DIM
KV_WIDTH = N_KV_HEADS * HEAD_DIM
WINDOW = 128
ATTN_SCALE = HEAD_DIM ** -0.5
ROPE_THETA = 10000.0
HYENA_WIDTH = D_MODEL // 2
FILTER_BANDS = 8
FILTER_FEAT = 1 + 2 * FILTER_BANDS
FILTER_FEAT_PAD = 32
FILTER_HIDDEN = 64
D_FF = -(-8 * D_MODEL // (3 * 256)) * 256
IN_WIDTH = ATTN_WIDTH + 2 * KV_WIDTH + 3 * HYENA_WIDTH
EPS = 1e-6
NEG_INF = -1e30

V7X_VMEM_BYTES = 64 * 1024 * 1024
VMEM_LIMIT = V7X_VMEM_BYTES - 8 * 1024 * 1024
MXU_COLS = 256
FFN_CHUNK = MXU_COLS
ROW_TILE = 512
ATTN_Q_TILE = 128
ATTN_KEY_SPAN = ATTN_Q_TILE + 2 * WINDOW
HYENA_CH_TILE = 256

BF16 = jnp.bfloat16
F32 = jnp.float32


def _params(*semantics):
    return pltpu.CompilerParams(dimension_semantics=semantics, vmem_limit_bytes=VMEM_LIMIT)


def _const_spec(shape):
    zeros = (0,) * len(shape)
    return pl.BlockSpec(shape, lambda *_: zeros, pipeline_mode=pl.Buffered(1))


def _dot(a, b):
    return jnp.dot(a, b, preferred_element_type=F32)


def _rmsnorm(x, g):
    return x * lax.rsqrt(jnp.mean(x * x, axis=-1, keepdims=True) + EPS) * g


def _silu(x):
    return x * jax.nn.sigmoid(x)


def _mod_kernel(cond_ref, w_ref, b_ref, o_ref):
    a = _silu(cond_ref[...]).astype(BF16)
    o_ref[...] = _dot(a, w_ref[...].astype(BF16)) + b_ref[...]


def _modulation(cond, w_mod, b_mod):
    rows = cond.shape[0]
    n = w_mod.shape[1]
    tn = D_MODEL
    return pl.pallas_call(
        _mod_kernel,
        grid=(n // tn,),
        in_specs=[pl.BlockSpec((rows, D_MODEL), lambda j: (0, 0)),
                  pl.BlockSpec((D_MODEL, tn), lambda j: (0, j)),
                  pl.BlockSpec((1, tn), lambda j: (0, j))],
        out_specs=pl.BlockSpec((rows, tn), lambda j: (0, j)),
        out_shape=jax.ShapeDtypeStruct((rows, n), F32),
        compiler_params=_params("arbitrary"),
        name="modulation",
    )(cond, w_mod, b_mod.reshape(1, n))


def _dft_tables(L):
    bs = 1 << (int(math.log2(L)) // 2)
    nb = L // bs
    t = np.arange(L, dtype=np.int64)
    theta = math.pi / L
    ang_a = ((np.arange(nb, dtype=np.int64)[:, None] * bs * t[None, :]) % (2 * L)) * theta
    ang_b = ((np.arange(bs, dtype=np.int64)[:, None] * t[None, :]) % (2 * L)) * theta
    ca = jnp.asarray(np.cos(ang_a), F32)[:, None, :]
    sa = jnp.asarray(np.sin(ang_a), F32)[:, None, :]
    cb = jnp.asarray(np.cos(ang_b), F32)[None, :, :]
    sb = jnp.asarray(np.sin(ang_b), F32)[None, :, :]
    cos_t = (ca * cb - sa * sb).reshape(L, L).astype(BF16)
    sin_t = (sa * cb + ca * sb).reshape(L, L).astype(BF16)
    return cos_t, sin_t


def _spectrum_kernel(feat_ref, w1_ref, b1_ref, f1_ref, w2_ref, b2_ref, f2_ref, w3_ref, b3_ref,
                     decay_ref, tn_ref, cos_ref, sin_ref, hr_ref, hi_ref, hn_ref, a_s, b_s, *, L, tk):
    i = pl.program_id(0)
    hi_prec = lax.Precision.HIGHEST

    @pl.when(i == 0)
    def _():
        h = jnp.dot(feat_ref[...], w1_ref[...], precision=hi_prec, preferred_element_type=F32)
        h = jnp.sin(f1_ref[...] * (h + b1_ref[...]))
        h = jnp.dot(h, w2_ref[...], precision=hi_prec, preferred_element_type=F32)
        h = jnp.sin(f2_ref[...] * (h + b2_ref[...]))
        h = jnp.dot(h, w3_ref[...], precision=hi_prec, preferred_element_type=F32) + b3_ref[...]
        h = h * jnp.exp(-tn_ref[...] * jnp.abs(decay_ref[...]))
        fwd = h[:, :HYENA_WIDTH]
        bwd = h[:, HYENA_WIDTH:]
        row = lax.broadcasted_iota(jnp.int32, (L, HYENA_WIDTH), 0)
        bwd = jnp.where(row == 0, 0.0, bwd)
        even = fwd + bwd
        a_s[...] = even.astype(BF16)
        b_s[...] = (fwd - bwd).astype(BF16)
        sign = jnp.where((row & 1) == 0, 1.0, -1.0)
        hn_ref[...] = jnp.sum(even * sign, axis=0, keepdims=True) * (1.0 / (2 * L))

    krow = lax.broadcasted_iota(jnp.int32, (tk, HYENA_WIDTH), 0) + i * tk
    wk = jnp.where(krow == 0, 1.0, 2.0) * (1.0 / (2 * L))
    hr_ref[...] = _dot(cos_ref[...], a_s[...]) * wk
    hi_ref[...] = _dot(sin_ref[...], b_s[...]) * (-wk)


def _spectrum(L, feat, tnorm, cos_t, sin_t, w1, b1, f1, w2, b2, f2, w3, b3, decay):
    tk = min(L, 512)
    C = HYENA_WIDTH
    small = [feat, w1, b1, f1, w2, b2, f2, w3, b3, decay, tnorm]
    return pl.pallas_call(
        functools.partial(_spectrum_kernel, L=L, tk=tk),
        grid=(L // tk,),
        in_specs=[_const_spec(a.shape) for a in small]
                 + [pl.BlockSpec((tk, L), lambda i: (i, 0)), pl.BlockSpec((tk, L), lambda i: (i, 0))],
        out_specs=[pl.BlockSpec((tk, C), lambda i: (i, 0)), pl.BlockSpec((tk, C), lambda i: (i, 0)),
                   pl.BlockSpec((1, C), lambda i: (0, 0))],
        out_shape=[jax.ShapeDtypeStruct((L, C), F32), jax.ShapeDtypeStruct((L, C), F32),
                   jax.ShapeDtypeStruct((1, C), F32)],
        scratch_shapes=[pltpu.VMEM((L, C), BF16), pltpu.VMEM((L, C), BF16)],
        compiler_params=_params("arbitrary"),
        name=f"spectrum_{L}",
    )(*small, cos_t, sin_t)


def _modulated_norm(x, g, shift, scale):
    return _rmsnorm(x, g) * (1.0 + scale) + shift


def _rope(x, cos, sin_signed):
    lane = lax.broadcasted_iota(jnp.int32, x.shape, 1)
    swapped = jnp.where((lane & 63) < 32,
                        pltpu.roll(x, HEAD_DIM - 32, axis=1),
                        pltpu.roll(x, 32, axis=1))
    return x * cos + swapped * sin_signed


def _inproj_kernel(*refs, rope):
    if rope:
        x_ref, mod_ref, g_ref, w_ref, cos_ref, sin_ref, q_ref, k_ref, v_ref, vt_ref, hy_ref = refs
    else:
        x_ref, mod_ref, g_ref, w_ref, q_ref, k_ref, v_ref, vt_ref, hy_ref = refs
    m = mod_ref[0]
    h = _modulated_norm(x_ref[...], g_ref[...], m[:, 0:D_MODEL], m[:, D_MODEL:2 * D_MODEL]).astype(BF16)

    def heads(y, out_ref, n, scale):
        if not rope:
            out_ref[...] = (y if scale == 1.0 else y * scale).astype(out_ref.dtype)
            return
        cos, sin = cos_ref[...], sin_ref[...]
        if scale != 1.0:
            cos, sin = cos * scale, sin * scale
        for hd in range(n):
            cols = slice(hd * HEAD_DIM, (hd + 1) * HEAD_DIM)
            out_ref[:, cols] = _rope(y[:, cols], cos, sin).astype(out_ref.dtype)

    heads(_dot(h, w_ref[:, :ATTN_WIDTH]), q_ref, N_HEADS, ATTN_SCALE)
    kv = _dot(h, w_ref[:, ATTN_WIDTH:ATTN_WIDTH + 2 * KV_WIDTH])
    heads(kv[:, :KV_WIDTH], k_ref, N_KV_HEADS, 1.0)
    v = kv[:, KV_WIDTH:]
    v_ref[...] = v.astype(v_ref.dtype)
    vt_ref[...] = v.T.astype(vt_ref.dtype)
    hy_ref[...] = _dot(h, w_ref[:, ATTN_WIDTH + 2 * KV_WIDTH:]).astype(hy_ref.dtype)


def _inproj(x, mod, g_pre, w_in, L, rope_tabs, kv_dtype):
    R = x.shape[0]
    tm = ROW_TILE
    steps_per_mod = (R // mod.shape[0]) // tm
    row = lambda i: (i, 0)
    in_specs = [pl.BlockSpec((tm, D_MODEL), row),
                pl.BlockSpec((1, 1, mod.shape[2]), lambda i: (i // steps_per_mod, 0, 0)),
                _const_spec(g_pre.shape), _const_spec(w_in.shape)]
    args = [x, mod, g_pre, w_in]
    if rope_tabs is not None:
        steps_per_seq = L // tm
        tab = pl.BlockSpec((tm, HEAD_DIM), lambda i: (i % steps_per_seq, 0))
        in_specs += [tab, tab]
        args += list(rope_tabs)
    return pl.pallas_call(
        functools.partial(_inproj_kernel, rope=rope_tabs is not None),
        grid=(R // tm,),
        in_specs=in_specs,
        out_specs=[pl.BlockSpec((tm, ATTN_WIDTH), row), pl.BlockSpec((tm, KV_WIDTH), row),
                   pl.BlockSpec((tm, KV_WIDTH), row), pl.BlockSpec((KV_WIDTH, tm), lambda i: (0, i)),
                   pl.BlockSpec((tm, 3 * HYENA_WIDTH), row)],
        out_shape=[jax.ShapeDtypeStruct((R, ATTN_WIDTH), BF16), jax.ShapeDtypeStruct((R, KV_WIDTH), kv_dtype),
                   jax.ShapeDtypeStruct((R, KV_WIDTH), kv_dtype), jax.ShapeDtypeStruct((KV_WIDTH, R), BF16),
                   jax.ShapeDtypeStruct((R, 3 * HYENA_WIDTH), BF16)],
        compiler_params=_params("arbitrary"),
        name=f"inproj_{L}",
    )(*args)


def _rope_tables(L):
    rows = L // GRID_W
    row_ids = jnp.repeat(jnp.arange(rows), GRID_W)
    col_ids = jnp.tile(jnp.arange(GRID_W), rows)
    half = HEAD_DIM // 2
    inv_freq = ROPE_THETA ** (-jnp.arange(0, half, 2, dtype=F32) / half)
    cos_parts, sin_parts = [], []
    for pos in (row_ids, col_ids):
        ang = pos.astype(F32)[:, None] * inv_freq[None, :]
        cos_parts += [jnp.cos(ang), jnp.cos(ang)]
        sin_parts += [-jnp.sin(ang), jnp.sin(ang)]
    return jnp.concatenate(cos_parts, axis=-1), jnp.concatenate(sin_parts, axis=-1)


def _band_bias():
    key = np.arange(ATTN_KEY_SPAN)[:, None]
    qry = (np.arange(GROUP * ATTN_Q_TILE) % ATTN_Q_TILE)[None, :]
    tabs = [np.where(np.abs(key - off - qry) <= WINDOW, 0.0, NEG_INF) for off in (0, WINDOW, 2 * WINDOW)]
    return jnp.asarray(np.stack(tabs), F32)


def _attn_kernel(*refs, windowed, L, nblk):
    if windowed:
        sink_ref, q_ref, k_ref, vt_ref, ck_ref, cvt_ref, bias_ref, o_ref = refs
    else:
        sink_ref, q_ref, k_ref, vt_ref, o_ref = refs
    tq = ATTN_Q_TILE
    nt = (((1,), (1,)), ((), ()))
    for jb in range(nblk):
        rows = slice(jb * tq, (jb + 1) * tq)
        if windowed:
            qs = (pl.program_id(1) * nblk + jb) * tq
            start = pl.multiple_of(jnp.clip(qs - WINDOW, 0, L - ATTN_KEY_SPAN), WINDOW)
            bias = bias_ref[(qs - start) // WINDOW]
        for kk in range(N_KV_HEADS):
            heads = [kk * GROUP + g for g in range(GROUP)]
            kv = slice(kk * HEAD_DIM, (kk + 1) * HEAD_DIM)
            q4 = jnp.concatenate([q_ref[rows, hd * HEAD_DIM:(hd + 1) * HEAD_DIM] for hd in heads], axis=0)
            sink = jnp.concatenate([jnp.full((1, tq), sink_ref[hd], F32) for hd in heads], axis=1)
            if windowed:
                s_w = lax.dot_general(k_ref[pl.ds(start, ATTN_KEY_SPAN), kv], q4, nt,
                                      preferred_element_type=F32) + bias
                s_c = lax.dot_general(ck_ref[:, kv].astype(BF16), q4, nt, preferred_element_type=F32)
                m = jnp.maximum(jnp.maximum(jnp.max(s_w, axis=0, keepdims=True),
                                            jnp.max(s_c, axis=0, keepdims=True)), sink)
                p_w = jnp.exp(s_w - m)
                p_c = jnp.exp(s_c - m)
                denom = (jnp.sum(p_w, axis=0, keepdims=True) + jnp.sum(p_c, axis=0, keepdims=True)
                         + jnp.exp(sink - m))
                o = (_dot(vt_ref[kv, pl.ds(start, ATTN_KEY_SPAN)], p_w.astype(BF16))
                     + _dot(cvt_ref[kv, :].astype(BF16), p_c.astype(BF16)))
            else:
                s = lax.dot_general(k_ref[:, kv].astype(BF16), q4, nt, preferred_element_type=F32)
                m = jnp.maximum(jnp.max(s, axis=0, keepdims=True), sink)
                p = jnp.exp(s - m)
                denom = jnp.sum(p, axis=0, keepdims=True) + jnp.exp(sink - m)
                o = _dot(vt_ref[kv, :], p.astype(BF16))
            o = (o * (1.0 / denom)).T
            for g, hd in enumerate(heads):
                o_ref[rows, hd * HEAD_DIM:(hd + 1) * HEAD_DIM] = o[g * tq:(g + 1) * tq].astype(o_ref.dtype)


def _attention(q, k, vt, sink, B, L, ctx_k=None, ctx_vt=None):
    windowed = ctx_k is not None
    nblk = min(L // ATTN_Q_TILE, 4)
    tstep = nblk * ATTN_Q_TILE
    nq = L // tstep
    sink_spec = pl.BlockSpec(memory_space=pltpu.SMEM)
    q_spec = pl.BlockSpec((tstep, ATTN_WIDTH), lambda b, i: (b * nq + i, 0))
    in_specs = [sink_spec, q_spec,
                pl.BlockSpec((L, KV_WIDTH), lambda b, i: (b, 0)),
                pl.BlockSpec((KV_WIDTH, L), lambda b, i: (0, b))]
    args = [sink, q, k, vt]
    if windowed:
        assert L >= ATTN_KEY_SPAN + WINDOW
        P = ctx_k.shape[0] // B
        bias = _band_bias()
        in_specs += [pl.BlockSpec((P, KV_WIDTH), lambda b, i: (b, 0)),
                     pl.BlockSpec((KV_WIDTH, P), lambda b, i: (b, 0)),
                     _const_spec(bias.shape)]
        args += [ctx_k, ctx_vt, bias]
    return pl.pallas_call(
        functools.partial(_attn_kernel, windowed=windowed, L=L, nblk=nblk),
        grid=(B, nq),
        in_specs=in_specs,
        out_specs=q_spec,
        out_shape=jax.ShapeDtypeStruct((B * L, ATTN_WIDTH), BF16),
        compiler_params=_params("arbitrary", "arbitrary"),
        name=f"attention_{L}",
    )(*args)


def _hyena_kernel(x0_ref, x1_ref, v_ref, cw0_ref, cw1_ref, cwv_ref, cb0_ref, cb1_ref, cbv_ref,
                  hr_ref, hi_ref, hn_ref, skip_ref, cos_ref, sin_ref, o_ref, *, L):
    tc = o_ref.shape[1]
    row = lax.broadcasted_iota(jnp.int32, (L, tc), 0)
    first = row == 0
    last = row == L - 1

    def short_conv(u_ref, w_ref, b_ref):
        u = u_ref[...].astype(F32)
        prev = jnp.where(first, 0.0, pltpu.roll(u, 1, axis=0))
        nxt = jnp.where(last, 0.0, pltpu.roll(u, L - 1, axis=0))
        return prev * w_ref[0:1, :] + u * w_ref[1:2, :] + nxt * w_ref[2:3, :] + b_ref[...]

    z = short_conv(v_ref, cwv_ref, cbv_ref) * short_conv(x1_ref, cw1_ref, cb1_ref)
    zb = z.astype(BF16)
    sign = jnp.where((row & 1) == 0, 1.0, -1.0)
    z_nyq = jnp.sum(z * sign, axis=0, keepdims=True)
    zc = _dot(cos_ref[...], zb)
    zs = _dot(sin_ref[...], zb)
    hr = hr_ref[...]
    hi = hi_ref[...]
    wr = (zc * hr + zs * hi).astype(BF16)
    wi = (zc * hi - zs * hr).astype(BF16)
    y = _dot(cos_ref[...], wr) - _dot(sin_ref[...], wi)
    y = y + sign * (z_nyq * hn_ref[...]) + z * skip_ref[...]
    o_ref[...] = (short_conv(x0_ref, cw0_ref, cb0_ref) * y).astype(o_ref.dtype)


def _hyena(hy, conv_w, conv_b, hr, hi, hn, skip, cos_t, sin_t, B, L):
    C = HYENA_WIDTH
    tc = HYENA_CH_TILE
    nj = C // tc

    def part(p, rows):
        return pl.BlockSpec((rows, tc), lambda b, j: (b if rows == L else 0, p * nj + j))

    spec_c = pl.BlockSpec((L, tc), lambda b, j: (0, j))
    vec_c = pl.BlockSpec((1, tc), lambda b, j: (0, j))
    conv_b = conv_b.reshape(1, 3 * C)
    return pl.pallas_call(
        functools.partial(_hyena_kernel, L=L),
        grid=(B, nj),
        in_specs=[part(0, L), part(1, L), part(2, L),
                  part(0, 3), part(1, 3), part(2, 3),
                  part(0, 1), part(1, 1), part(2, 1),
                  spec_c, spec_c, vec_c, vec_c,
                  _const_spec((L, L)), _const_spec((L, L))],
        out_specs=pl.BlockSpec((L, tc), lambda b, j: (b, j)),
        out_shape=jax.ShapeDtypeStruct((B * L, C), BF16),
        compiler_params=_params("arbitrary", "arbitrary"),
        name=f"hyena_{L}",
    )(hy, hy, hy, conv_w, conv_w, conv_w, conv_b, conv_b, conv_b,
      hr, hi, hn, skip.reshape(1, C), cos_t, sin_t)


def _merge_ffn_kernel(x_ref, attn_ref, hyo_ref, mod_ref, g_pre_ref, g_post_ref, g_fpre_ref, g_fpost_ref,
                      wg_ref, bg_ref, wpa_ref, wph_ref, wo_ref, wup_ref, wdn_ref, o_ref):
    D = D_MODEL
    m = mod_ref[0]
    sh1, sc1, g1, sh2, sc2, g2 = [m[:, j * D:(j + 1) * D] for j in range(6)]
    x = x_ref[...]
    h = _modulated_norm(x, g_pre_ref[...], sh1, sc1).astype(BF16)
    gate_a = jax.nn.sigmoid(_dot(h, wg_ref[:, :D]) + bg_ref[:, :D])
    merged = gate_a * _dot(attn_ref[...], wpa_ref[...])
    gate_h = jax.nn.sigmoid(_dot(h, wg_ref[:, D:]) + bg_ref[:, D:])
    merged = merged + gate_h * _dot(hyo_ref[...], wph_ref[...])
    mix = _dot(merged.astype(BF16), wo_ref[...])
    x = x + g1 * _rmsnorm(mix, g_post_ref[...])
    h2 = _modulated_norm(x, g_fpre_ref[...], sh2, sc2).astype(BF16)
    f = jnp.zeros(x.shape, F32)
    for c0 in range(0, D_FF, FFN_CHUNK):
        gt = _dot(h2, wup_ref[:, c0:c0 + FFN_CHUNK])
        up = _dot(h2, wup_ref[:, D_FF + c0:D_FF + c0 + FFN_CHUNK])
        f = f + _dot((_silu(gt) * up).astype(BF16), wdn_ref[c0:c0 + FFN_CHUNK, :])
    o_ref[...] = x + g2 * _rmsnorm(f, g_fpost_ref[...])


def _merge_ffn(x, attn, hyo, mod, norms, w_gate, b_gate, w_pa, w_ph, w_o, w_up, w_down, L):
    R = x.shape[0]
    tm = ROW_TILE
    steps_per_mod = (R // mod.shape[0]) // tm
    row = lambda i: (i, 0)
    consts = list(norms) + [w_gate, b_gate, w_pa, w_ph, w_o, w_up, w_down]
    return pl.pallas_call(
        _merge_ffn_kernel,
        grid=(R // tm,),
        in_specs=[pl.BlockSpec((tm, D_MODEL), row), pl.BlockSpec((tm, ATTN_WIDTH), row),
                  pl.BlockSpec((tm, HYENA_WIDTH), row),
                  pl.BlockSpec((1, 1, mod.shape[2]), lambda i: (i // steps_per_mod, 0, 0))]
                 + [_const_spec(a.shape) for a in consts],
        out_specs=pl.BlockSpec((tm, D_MODEL), row),
        out_shape=jax.ShapeDtypeStruct((R, D_MODEL), F32),
        compiler_params=_params("arbitrary"),
        name=f"merge_ffn_{L}",
    )(x, attn, hyo, mod, *consts)


def _filter_features(L):
    t = jnp.arange(L, dtype=F32)
    t_norm = t / L
    bands = jnp.linspace(1e-4, FILTER_BANDS - 1, FILTER_BANDS).astype(F32)
    ang = (2.0 * math.pi * t / L)[:, None] * bands[None, :]
    feat = jnp.concatenate([t_norm[:, None], jnp.cos(ang), jnp.sin(ang)], axis=-1)
    feat = jnp.pad(feat, ((0, 0), (0, FILTER_FEAT_PAD - FILTER_FEAT)))
    return feat, t_norm[:, None]


def _layer(x, mod, p, B, L, latent, ctx_k, ctx_v):
    cos_t, sin_t = _dft_tables(L)
    feat, tnorm = _filter_features(L)
    hr, hi, hn = _spectrum(L, feat, tnorm, cos_t, sin_t, p['filt_w1'], p['filt_b1'], p['filt_freq1'],
                           p['filt_w2'], p['filt_b2'], p['filt_freq2'], p['filt_w3'], p['filt_b3'],
                           p['filt_decay'])
    rope_tabs = _rope_tables(L) if latent else None
    q, k, v, vt, hy = _inproj(x, mod, p['norm_mix_pre'], p['w_in'], L, rope_tabs, BF16 if latent else F32)
    attn = _attention(q, k, vt, p['attn_sink'], B, L, ctx_k, ctx_v)
    hyo = _hyena(hy, p['conv_w'], p['conv_b'], hr, hi, hn, p['hyena_skip'], cos_t, sin_t, B, L)
    norms = [p['norm_mix_pre'], p['norm_mix_post'], p['norm_ffn_pre'], p['norm_ffn_post']]
    y = _merge_ffn(x, attn, hyo, mod, norms, p['w_gate'], p['b_gate'], p['w_pa'], p['w_ph'], p['w_o'],
                   p['w_up'], p['w_down'], L)
    return y, k, v


def kernel(x_prompt, x_sample, c, cache_k, cache_v, c_ctx, norm_mix_pre, norm_mix_post, norm_ffn_pre, norm_ffn_post, w_mod, b_mod, w_in, attn_sink, conv_w, conv_b, filt_w1, filt_b1, filt_freq1, filt_w2, filt_b2, filt_freq2, filt_w3, filt_b3, filt_decay, hyena_skip, w_pa, w_ph, w_gate, b_gate, w_o, w_up, w_down):
    Bp, Lp, D = x_prompt.shape
    Bs, Ls, _ = x_sample.shape
    depth = w_in.shape[0]
    past = cache_k.shape[2]
    yp = x_prompt.reshape(Bp * Lp, D)
    ys = x_sample.reshape(Bs * Ls, D)
    cond = jnp.concatenate([c_ctx[None, :], c, jnp.zeros((8 - 1 - Bs, D), F32)], axis=0)
    new_k, new_v = [], []
    row2 = lambda a: a.reshape(1, -1)
    for l in range(depth):
        w1 = jnp.pad(filt_w1[l], ((0, FILTER_FEAT_PAD - FILTER_FEAT), (0, 0)))
        p = {
            'norm_mix_pre': row2(norm_mix_pre[l]), 'norm_mix_post': row2(norm_mix_post[l]),
            'norm_ffn_pre': row2(norm_ffn_pre[l]), 'norm_ffn_post': row2(norm_ffn_post[l]),
            'w_in': w_in[l].astype(BF16), 'attn_sink': attn_sink[l],
            'conv_w': conv_w[l], 'conv_b': conv_b[l],
            'filt_w1': w1, 'filt_b1': row2(filt_b1[l]), 'filt_freq1': row2(filt_freq1[l]),
            'filt_w2': filt_w2[l], 'filt_b2': row2(filt_b2[l]), 'filt_freq2': row2(filt_freq2[l]),
            'filt_w3': filt_w3[l], 'filt_b3': row2(filt_b3[l]), 'filt_decay': row2(filt_decay[l]),
            'hyena_skip': hyena_skip[l], 'w_pa': w_pa[l].astype(BF16), 'w_ph': w_ph[l].astype(BF16),
            'w_gate': w_gate[l].astype(BF16), 'b_gate': row2(b_gate[l]), 'w_o': w_o[l].astype(BF16),
            'w_up': w_up[l].astype(BF16), 'w_down': w_down[l].astype(BF16),
        }
        mod = _modulation(cond, w_mod[l], b_mod[l])
        mod_p = mod[0:1].reshape(1, 1, 6 * D)
        mod_s = mod[1:1 + Bs].reshape(Bs, 1, 6 * D)
        yp, k_ctx, v_ctx = _layer(yp, mod_p, p, Bp, Lp, False, None, None)
        new_k.append(k_ctx.reshape(Bp, Lp, N_KV_HEADS, HEAD_DIM))
        new_v.append(v_ctx.reshape(Bp, Lp, N_KV_HEADS, HEAD_DIM))
        ck = cache_k[:, l].reshape(Bs * past, KV_WIDTH)
        cv = jnp.swapaxes(cache_v[:, l].reshape(Bs, past, KV_WIDTH), 1, 2).reshape(Bs * KV_WIDTH, past)
        ys, _, _ = _layer(ys, mod_s, p, Bs, Ls, True, ck, cv)
    return (yp.reshape(Bp, Lp, D), ys.reshape(Bs, Ls, D),
            jnp.stack(new_k, axis=1), jnp.stack(new_v, axis=1))
```

```python
import functools
import math

import numpy as np
import jax
import jax.numpy as jnp
from jax import lax
from jax.experimental import pallas as pl
from jax.experimental.pallas import tpu as pltpu

D_MODEL = 1024
GRID_W = 64
N_HEADS = 8
N_KV_HEADS = 2
HEAD_DIM = 128
GROUP = N_HEADS // N_KV_HEADS
ATTN_WIDTH = N_HEADS * HEAD_DIM
KV_WIDTH = N_KV_HEADS * HEAD_DIM
WINDOW = 128
ATTN_SCALE = HEAD_DIM ** -0.5
LOG2E = math.log2(math.e)
ROPE_THETA = 10000.0
HYENA_WIDTH = D_MODEL // 2
FILTER_BANDS = 8
FILTER_FEAT = 1 + 2 * FILTER_BANDS
FILTER_FEAT_PAD = 32
FILTER_HIDDEN = 64
D_FF = -(-8 * D_MODEL // (3 * 256)) * 256
IN_WIDTH = ATTN_WIDTH + 2 * KV_WIDTH + 3 * HYENA_WIDTH
EPS = 1e-6
NEG_INF = -1e30

V7X_VMEM_BYTES = 64 * 1024 * 1024
VMEM_LIMIT = V7X_VMEM_BYTES - 8 * 1024 * 1024
MXU_COLS = 256
FFN_CHUNK = MXU_COLS
ROW_TILE = 512
ATTN_Q_TILE = 128
ATTN_KEY_SPAN = ATTN_Q_TILE + 2 * WINDOW
HYENA_CH_TILE = 256

BF16 = jnp.bfloat16
F32 = jnp.float32


def _params(*semantics):
    return pltpu.CompilerParams(dimension_semantics=semantics, vmem_limit_bytes=VMEM_LIMIT)


def _const_spec(shape):
    zeros = (0,) * len(shape)
    return pl.BlockSpec(shape, lambda *_: zeros, pipeline_mode=pl.Buffered(1))


def _dot(a, b):
    return jnp.dot(a, b, preferred_element_type=F32)


def _rmsnorm(x, g):
    return x * lax.rsqrt(jnp.mean(x * x, axis=-1, keepdims=True) + EPS) * g


def _silu(x):
    return x * jax.nn.sigmoid(x)


def _mod_kernel(cond_ref, w_ref, b_ref, o_ref):
    a = _silu(cond_ref[...]).astype(BF16)
    o_ref[...] = _dot(a, w_ref[...].astype(BF16)) + b_ref[...]


def _modulation(cond, w_mod, b_mod):
    rows = cond.shape[0]
    n = w_mod.shape[1]
    tn = D_MODEL
    return pl.pallas_call(
        _mod_kernel,
        grid=(n // tn,),
        in_specs=[pl.BlockSpec((rows, D_MODEL), lambda j: (0, 0)),
                  pl.BlockSpec((D_MODEL, tn), lambda j: (0, j)),
                  pl.BlockSpec((1, tn), lambda j: (0, j))],
        out_specs=pl.BlockSpec((rows, tn), lambda j: (0, j)),
        out_shape=jax.ShapeDtypeStruct((rows, n), F32),
        compiler_params=_params("arbitrary"),
        name="modulation",
    )(cond, w_mod, b_mod.reshape(1, n))


def _dft_tables(L):
    bs = 1 << (int(math.log2(L)) // 2)
    nb = L // bs
    t = np.arange(L, dtype=np.int64)
    theta = math.pi / L
    ang_a = ((np.arange(nb, dtype=np.int64)[:, None] * bs * t[None, :]) % (2 * L)) * theta
    ang_b = ((np.arange(bs, dtype=np.int64)[:, None] * t[None, :]) % (2 * L)) * theta
    ca = jnp.asarray(np.cos(ang_a), F32)[:, None, :]
    sa = jnp.asarray(np.sin(ang_a), F32)[:, None, :]
    cb = jnp.asarray(np.cos(ang_b), F32)[None, :, :]
    sb = jnp.asarray(np.sin(ang_b), F32)[None, :, :]
    cos_t = (ca * cb - sa * sb).reshape(L, L).astype(BF16)
    sin_t = (sa * cb + ca * sb).reshape(L, L).astype(BF16)
    return cos_t, sin_t


def _spectrum_kernel(feat_ref, w1_ref, b1_ref, f1_ref, w2_ref, b2_ref, f2_ref, w3_ref, b3_ref,
                     decay_ref, tn_ref, cos_ref, sin_ref, hr_ref, hi_ref, hn_ref, a_s, b_s, *, L, tk):
    i = pl.program_id(0)
    hi_prec = lax.Precision.HIGHEST

    @pl.when(i == 0)
    def _():
        h = jnp.dot(w1_ref[...], feat_ref[...], precision=hi_prec, preferred_element_type=F32)
        h = jnp.sin(f1_ref[...] * (h + b1_ref[...]))
        h = jnp.dot(w2_ref[...], h, precision=hi_prec, preferred_element_type=F32)
        h = jnp.sin(f2_ref[...] * (h + b2_ref[...]))
        h = _dot(h.T.astype(BF16), w3_ref[...].astype(BF16)) + b3_ref[...]
        h = h * jnp.exp(-tn_ref[...] * jnp.abs(decay_ref[...]))
        fwd = h[:, :HYENA_WIDTH]
        bwd = h[:, HYENA_WIDTH:]
        row = lax.broadcasted_iota(jnp.int32, (L, HYENA_WIDTH), 0)
        bwd = jnp.where(row == 0, 0.0, bwd)
        even = fwd + bwd
        a_s[...] = even.astype(BF16)
        b_s[...] = (fwd - bwd).astype(BF16)
        sign = jnp.where((row & 1) == 0, 1.0, -1.0)
        hn_ref[...] = jnp.sum(even * sign, axis=0, keepdims=True) * (1.0 / (2 * L))

    krow = lax.broadcasted_iota(jnp.int32, (tk, HYENA_WIDTH), 0) + i * tk
    wk = jnp.where(krow == 0, 1.0, 2.0) * (1.0 / (2 * L))
    hr_ref[...] = _dot(cos_ref[...], a_s[...]) * wk
    hi_ref[...] = _dot(sin_ref[...], b_s[...]) * (-wk)


def _spectrum(L, feat, tnorm, cos_t, sin_t, w1, b1, f1, w2, b2, f2, w3, b3, decay):
    tk = min(L, 512)
    C = HYENA_WIDTH
    small = [feat, w1, b1, f1, w2, b2, f2, w3, b3, decay, tnorm]
    return pl.pallas_call(
        functools.partial(_spectrum_kernel, L=L, tk=tk),
        grid=(L // tk,),
        in_specs=[_const_spec(a.shape) for a in small]
                 + [pl.BlockSpec((tk, L), lambda i: (i, 0)), pl.BlockSpec((tk, L), lambda i: (i, 0))],
        out_specs=[pl.BlockSpec((tk, C), lambda i: (i, 0)), pl.BlockSpec((tk, C), lambda i: (i, 0)),
                   pl.BlockSpec((1, C), lambda i: (0, 0))],
        out_shape=[jax.ShapeDtypeStruct((L, C), F32), jax.ShapeDtypeStruct((L, C), F32),
                   jax.ShapeDtypeStruct((1, C), F32)],
        scratch_shapes=[pltpu.VMEM((L, C), BF16), pltpu.VMEM((L, C), BF16)],
        compiler_params=_params("arbitrary"),
        name=f"spectrum_{L}",
    )(*small, cos_t, sin_t)


def _modulated_norm(x, g, shift, scale):
    return _rmsnorm(x, g) * (1.0 + scale) + shift


def _rope(x, cos, sin_signed):
    lane = lax.broadcasted_iota(jnp.int32, x.shape, 1)
    swapped = jnp.where((lane & 63) < 32,
                        pltpu.roll(x, HEAD_DIM - 32, axis=1),
                        pltpu.roll(x, 32, axis=1))
    return x * cos + swapped * sin_signed


def _inproj_kernel(*refs, rope):
    if rope:
        x_ref, mod_ref, g_ref, w_ref, cos_ref, sin_ref, q_ref, k_ref, v_ref, vt_ref, hy_ref = refs
    else:
        x_ref, mod_ref, g_ref, w_ref, q_ref, k_ref, v_ref, vt_ref, hy_ref = refs
    m = mod_ref[0]
    h = _modulated_norm(x_ref[...], g_ref[...], m[:, 0:D_MODEL], m[:, D_MODEL:2 * D_MODEL]).astype(BF16)

    def heads(y, out_ref, n, scale):
        if not rope:
            out_ref[...] = (y if scale == 1.0 else y * scale).astype(out_ref.dtype)
            return
        cos, sin = cos_ref[...], sin_ref[...]
        if scale != 1.0:
            cos, sin = cos * scale, sin * scale
        for hd in range(n):
            cols = slice(hd * HEAD_DIM, (hd + 1) * HEAD_DIM)
            out_ref[:, cols] = _rope(y[:, cols], cos, sin).astype(out_ref.dtype)

    heads(_dot(h, w_ref[:, :ATTN_WIDTH]), q_ref, N_HEADS, ATTN_SCALE * LOG2E)
    kv = _dot(h, w_ref[:, ATTN_WIDTH:ATTN_WIDTH + 2 * KV_WIDTH])
    heads(kv[:, :KV_WIDTH], k_ref, N_KV_HEADS, 1.0)
    v = kv[:, KV_WIDTH:]
    v_ref[...] = v.astype(v_ref.dtype)
    vt_ref[...] = v.T.astype(vt_ref.dtype)
    hy_ref[...] = _dot(h, w_ref[:, ATTN_WIDTH + 2 * KV_WIDTH:]).astype(hy_ref.dtype)


def _inproj(x, mod, g_pre, w_in, L, rope_tabs, kv_dtype):
    R = x.shape[0]
    tm = ROW_TILE
    steps_per_mod = (R // mod.shape[0]) // tm
    row = lambda i: (i, 0)
    in_specs = [pl.BlockSpec((tm, D_MODEL), row),
                pl.BlockSpec((1, 1, mod.shape[2]), lambda i: (i // steps_per_mod, 0, 0)),
                _const_spec(g_pre.shape), _const_spec(w_in.shape)]
    args = [x, mod, g_pre, w_in]
    if rope_tabs is not None:
        steps_per_seq = L // tm
        tab = pl.BlockSpec((tm, HEAD_DIM), lambda i: (i % steps_per_seq, 0))
        in_specs += [tab, tab]
        args += list(rope_tabs)
    return pl.pallas_call(
        functools.partial(_inproj_kernel, rope=rope_tabs is not None),
        grid=(R // tm,),
        in_specs=in_specs,
        out_specs=[pl.BlockSpec((tm, ATTN_WIDTH), row), pl.BlockSpec((tm, KV_WIDTH), row),
                   pl.BlockSpec((tm, KV_WIDTH), row), pl.BlockSpec((KV_WIDTH, tm), lambda i: (0, i)),
                   pl.BlockSpec((tm, 3 * HYENA_WIDTH), row)],
        out_shape=[jax.ShapeDtypeStruct((R, ATTN_WIDTH), BF16), jax.ShapeDtypeStruct((R, KV_WIDTH), kv_dtype),
                   jax.ShapeDtypeStruct((R, KV_WIDTH), kv_dtype), jax.ShapeDtypeStruct((KV_WIDTH, R), BF16),
                   jax.ShapeDtypeStruct((R, 3 * HYENA_WIDTH), BF16)],
        compiler_params=_params("arbitrary"),
        name=f"inproj_{L}",
    )(*args)


def _rope_tables(L):
    rows = L // GRID_W
    row_ids = jnp.repeat(jnp.arange(rows), GRID_W)
    col_ids = jnp.tile(jnp.arange(GRID_W), rows)
    half = HEAD_DIM // 2
    inv_freq = ROPE_THETA ** (-jnp.arange(0, half, 2, dtype=F32) / half)
    cos_parts, sin_parts = [], []
    for pos in (row_ids, col_ids):
        ang = pos.astype(F32)[:, None] * inv_freq[None, :]
        cos_parts += [jnp.cos(ang), jnp.cos(ang)]
        sin_parts += [-jnp.sin(ang), jnp.sin(ang)]
    return jnp.concatenate(cos_parts, axis=-1), jnp.concatenate(sin_parts, axis=-1)


def _band_bias():
    key = np.arange(ATTN_KEY_SPAN)[:, None]
    qry = (np.arange(GROUP * ATTN_Q_TILE) % ATTN_Q_TILE)[None, :]
    tabs = [np.where(np.abs(key - off - qry) <= WINDOW, 0.0, NEG_INF) for off in (0, WINDOW, 2 * WINDOW)]
    return jnp.asarray(np.stack(tabs), F32)


def _attn_kernel(*refs, windowed, L, nblk):
    if windowed:
        sink_ref, q_ref, k_ref, vt_ref, ck_ref, cvt_ref, bias_ref, o_ref = refs
    else:
        sink_ref, q_ref, k_ref, vt_ref, o_ref = refs
    tq = ATTN_Q_TILE
    nt = (((1,), (1,)), ((), ()))
    for jb in range(nblk):
        rows = slice(jb * tq, (jb + 1) * tq)
        if windowed:
            qs = (pl.program_id(1) * nblk + jb) * tq
            start = pl.multiple_of(jnp.clip(qs - WINDOW, 0, L - ATTN_KEY_SPAN), WINDOW)
            bias = bias_ref[(qs - start) // WINDOW]
        for kk in range(N_KV_HEADS):
            heads = [kk * GROUP + g for g in range(GROUP)]
            kv = slice(kk * HEAD_DIM, (kk + 1) * HEAD_DIM)
            q4 = jnp.concatenate([q_ref[rows, hd * HEAD_DIM:(hd + 1) * HEAD_DIM] for hd in heads], axis=0)
            sink = jnp.concatenate([jnp.full((1, tq), sink_ref[hd] * LOG2E, F32) for hd in heads], axis=1)
            if windowed:
                s_w = lax.dot_general(k_ref[pl.ds(start, ATTN_KEY_SPAN), kv], q4, nt,
                                      preferred_element_type=F32) + bias
                s_c = lax.dot_general(ck_ref[:, kv].astype(BF16), q4, nt, preferred_element_type=F32)
                m = jnp.maximum(jnp.maximum(jnp.max(s_w, axis=0, keepdims=True),
                                            jnp.max(s_c, axis=0, keepdims=True)), sink)
                p_w = jnp.exp2(s_w - m)
                p_c = jnp.exp2(s_c - m)
                denom = (jnp.sum(p_w, axis=0, keepdims=True) + jnp.sum(p_c, axis=0, keepdims=True)
                         + jnp.exp2(sink - m))
                o = (_dot(vt_ref[kv, pl.ds(start, ATTN_KEY_SPAN)], p_w.astype(BF16))
                     + _dot(cvt_ref[kv, :].astype(BF16), p_c.astype(BF16)))
            else:
                s = lax.dot_general(k_ref[:, kv].astype(BF16), q4, nt, preferred_element_type=F32)
                m = jnp.maximum(jnp.max(s, axis=0, keepdims=True), sink)
                p = jnp.exp2(s - m)
                denom = jnp.sum(p, axis=0, keepdims=True) + jnp.exp2(sink - m)
                o = _dot(vt_ref[kv, :], p.astype(BF16))
            o = (o * (1.0 / denom)).T
            for g, hd in enumerate(heads):
                o_ref[rows, hd * HEAD_DIM:(hd + 1) * HEAD_DIM] = o[g * tq:(g + 1) * tq].astype(o_ref.dtype)


def _attention(q, k, vt, sink, B, L, ctx_k=None, ctx_vt=None):
    windowed = ctx_k is not None
    nblk = min(L // ATTN_Q_TILE, 4)
    tstep = nblk * ATTN_Q_TILE
    nq = L // tstep
    sink_spec = pl.BlockSpec(memory_space=pltpu.SMEM)
    q_spec = pl.BlockSpec((tstep, ATTN_WIDTH), lambda b, i: (b * nq + i, 0))
    in_specs = [sink_spec, q_spec,
                pl.BlockSpec((L, KV_WIDTH), lambda b, i: (b, 0)),
                pl.BlockSpec((KV_WIDTH, L), lambda b, i: (0, b))]
    args = [sink, q, k, vt]
    if windowed:
        assert L >= ATTN_KEY_SPAN + WINDOW
        P = ctx_k.shape[0] // B
        bias = _band_bias()
        in_specs += [pl.BlockSpec((P, KV_WIDTH), lambda b, i: (b, 0)),
                     pl.BlockSpec((KV_WIDTH, P), lambda b, i: (b, 0)),
                     _const_spec(bias.shape)]
        args += [ctx_k, ctx_vt, bias]
    return pl.pallas_call(
        functools.partial(_attn_kernel, windowed=windowed, L=L, nblk=nblk),
        grid=(B, nq),
        in_specs=in_specs,
        out_specs=q_spec,
        out_shape=jax.ShapeDtypeStruct((B * L, ATTN_WIDTH), BF16),
        compiler_params=_params("arbitrary", "arbitrary"),
        name=f"attention_{L}",
    )(*args)


def _hyena_kernel(x0_ref, x1_ref, v_ref, cw0_ref, cw1_ref, cwv_ref, cb0_ref, cb1_ref, cbv_ref,
                  hr_ref, hi_ref, hn_ref, skip_ref, cos_ref, sin_ref, o_ref, *, L):
    tc = o_ref.shape[1]
    row = lax.broadcasted_iota(jnp.int32, (L, tc), 0)
    first = row == 0
    last = row == L - 1

    def short_conv(u_ref, w_ref, b_ref):
        u = u_ref[...].astype(F32)
        prev = jnp.where(first, 0.0, pltpu.roll(u, 1, axis=0))
        nxt = jnp.where(last, 0.0, pltpu.roll(u, L - 1, axis=0))
        return prev * w_ref[0:1, :] + u * w_ref[1:2, :] + nxt * w_ref[2:3, :] + b_ref[...]

    z = short_conv(v_ref, cwv_ref, cbv_ref) * short_conv(x1_ref, cw1_ref, cb1_ref)
    zb = z.astype(BF16)
    sign = jnp.where((row & 1) == 0, 1.0, -1.0)
    z_nyq = jnp.sum(z * sign, axis=0, keepdims=True)
    zc = _dot(cos_ref[...], zb)
    zs = _dot(sin_ref[...], zb)
    hr = hr_ref[...]
    hi = hi_ref[...]
    wr = (zc * hr + zs * hi).astype(BF16)
    wi = (zc * hi - zs * hr).astype(BF16)
    y = _dot(cos_ref[...], wr) - _dot(sin_ref[...], wi)
    y = y + sign * (z_nyq * hn_ref[...]) + z * skip_ref[...]
    o_ref[...] = (short_conv(x0_ref, cw0_ref, cb0_ref) * y).astype(o_ref.dtype)


def _hyena(hy, conv_w, conv_b, hr, hi, hn, skip, cos_t, sin_t, B, L):
    C = HYENA_WIDTH
    tc = HYENA_CH_TILE
    nj = C // tc

    def part(p, rows):
        return pl.BlockSpec((rows, tc), lambda b, j: (b if rows == L else 0, p * nj + j))

    spec_c = pl.BlockSpec((L, tc), lambda b, j: (0, j))
    vec_c = pl.BlockSpec((1, tc), lambda b, j: (0, j))
    conv_b = conv_b.reshape(1, 3 * C)
    return pl.pallas_call(
        functools.partial(_hyena_kernel, L=L),
        grid=(B, nj),
        in_specs=[part(0, L), part(1, L), part(2, L),
                  part(0, 3), part(1, 3), part(2, 3),
                  part(0, 1), part(1, 1), part(2, 1),
                  spec_c, spec_c, vec_c, vec_c,
                  _const_spec((L, L)), _const_spec((L, L))],
        out_specs=pl.BlockSpec((L, tc), lambda b, j: (b, j)),
        out_shape=jax.ShapeDtypeStruct((B * L, C), BF16),
        compiler_params=_params("arbitrary", "arbitrary"),
        name=f"hyena_{L}",
    )(hy, hy, hy, conv_w, conv_w, conv_w, conv_b, conv_b, conv_b,
      hr, hi, hn, skip.reshape(1, C), cos_t, sin_t)


def _merge_ffn_kernel(x_ref, attn_ref, hyo_ref, mod_ref, g_pre_ref, g_post_ref, g_fpre_ref, g_fpost_ref,
                      wg_ref, bg_ref, wpa_ref, wph_ref, wo_ref, wup_ref, wdn_ref, o_ref):
    D = D_MODEL
    m = mod_ref[0]
    sh1, sc1, g1, sh2, sc2, g2 = [m[:, j * D:(j + 1) * D] for j in range(6)]
    x = x_ref[...]
    h = _modulated_norm(x, g_pre_ref[...], sh1, sc1).astype(BF16)
    gate_a = jax.nn.sigmoid(_dot(h, wg_ref[:, :D]) + bg_ref[:, :D])
    merged = gate_a * _dot(attn_ref[...], wpa_ref[...])
    gate_h = jax.nn.sigmoid(_dot(h, wg_ref[:, D:]) + bg_ref[:, D:])
    merged = merged + gate_h * _dot(hyo_ref[...], wph_ref[...])
    mix = _dot(merged.astype(BF16), wo_ref[...])
    x = x + g1 * _rmsnorm(mix, g_post_ref[...])
    h2 = _modulated_norm(x, g_fpre_ref[...], sh2, sc2).astype(BF16)
    f = jnp.zeros(x.shape, F32)
    for c0 in range(0, D_FF, FFN_CHUNK):
        gt = _dot(h2, wup_ref[:, c0:c0 + FFN_CHUNK])
        up = _dot(h2, wup_ref[:, D_FF + c0:D_FF + c0 + FFN_CHUNK])
        f = f + _dot((_silu(gt) * up).astype(BF16), wdn_ref[c0:c0 + FFN_CHUNK, :])
    o_ref[...] = x + g2 * _rmsnorm(f, g_fpost_ref[...])


def _merge_ffn(x, attn, hyo, mod, norms, w_gate, b_gate, w_pa, w_ph, w_o, w_up, w_down, L):
    R = x.shape[0]
    tm = ROW_TILE
    steps_per_mod = (R // mod.shape[0]) // tm
    row = lambda i: (i, 0)
    consts = list(norms) + [w_gate, b_gate, w_pa, w_ph, w_o, w_up, w_down]
    return pl.pallas_call(
        _merge_ffn_kernel,
        grid=(R // tm,),
        in_specs=[pl.BlockSpec((tm, D_MODEL), row), pl.BlockSpec((tm, ATTN_WIDTH), row),
                  pl.BlockSpec((tm, HYENA_WIDTH), row),
                  pl.BlockSpec((1, 1, mod.shape[2]), lambda i: (i // steps_per_mod, 0, 0))]
                 + [_const_spec(a.shape) for a in consts],
        out_specs=pl.BlockSpec((tm, D_MODEL), row),
        out_shape=jax.ShapeDtypeStruct((R, D_MODEL), F32),
        compiler_params=_params("arbitrary"),
        name=f"merge_ffn_{L}",
    )(x, attn, hyo, mod, *consts)


def _filter_features(L):
    t = jnp.arange(L, dtype=F32)
    t_norm = t / L
    bands = jnp.linspace(1e-4, FILTER_BANDS - 1, FILTER_BANDS).astype(F32)
    ang = (2.0 * math.pi * t / L)[:, None] * bands[None, :]
    feat = jnp.concatenate([t_norm[:, None], jnp.cos(ang), jnp.sin(ang)], axis=-1)
    feat = jnp.pad(feat, ((0, 0), (0, FILTER_FEAT_PAD - FILTER_FEAT)))
    return feat.T, t_norm[:, None]


def _layer(x, mod, p, B, L, latent, ctx_k, ctx_v):
    cos_t, sin_t = _dft_tables(L)
    feat, tnorm = _filter_features(L)
    hr, hi, hn = _spectrum(L, feat, tnorm, cos_t, sin_t, p['filt_w1'], p['filt_b1'], p['filt_freq1'],
                           p['filt_w2'], p['filt_b2'], p['filt_freq2'], p['filt_w3'], p['filt_b3'],
                           p['filt_decay'])
    rope_tabs = _rope_tables(L) if latent else None
    q, k, v, vt, hy = _inproj(x, mod, p['norm_mix_pre'], p['w_in'], L, rope_tabs, BF16 if latent else F32)
    attn = _attention(q, k, vt, p['attn_sink'], B, L, ctx_k, ctx_v)
    hyo = _hyena(hy, p['conv_w'], p['conv_b'], hr, hi, hn, p['hyena_skip'], cos_t, sin_t, B, L)
    norms = [p['norm_mix_pre'], p['norm_mix_post'], p['norm_ffn_pre'], p['norm_ffn_post']]
    y = _merge_ffn(x, attn, hyo, mod, norms, p['w_gate'], p['b_gate'], p['w_pa'], p['w_ph'], p['w_o'],
                   p['w_up'], p['w_down'], L)
    return y, k, v


def kernel(x_prompt, x_sample, c, cache_k, cache_v, c_ctx, norm_mix_pre, norm_mix_post, norm_ffn_pre, norm_ffn_post, w_mod, b_mod, w_in, attn_sink, conv_w, conv_b, filt_w1, filt_b1, filt_freq1, filt_w2, filt_b2, filt_freq2, filt_w3, filt_b3, filt_decay, hyena_skip, w_pa, w_ph, w_gate, b_gate, w_o, w_up, w_down):
    Bp, Lp, D = x_prompt.shape
    Bs, Ls, _ = x_sample.shape
    depth = w_in.shape[0]
    past = cache_k.shape[2]
    yp = x_prompt.reshape(Bp * Lp, D)
    ys = x_sample.reshape(Bs * Ls, D)
    cond = jnp.concatenate([c_ctx[None, :], c, jnp.zeros((8 - 1 - Bs, D), F32)], axis=0)
    new_k, new_v = [], []
    row2 = lambda a: a.reshape(1, -1)
    col2 = lambda a: a.reshape(-1, 1)
    for l in range(depth):
        w1 = jnp.pad(filt_w1[l], ((0, FILTER_FEAT_PAD - FILTER_FEAT), (0, 0)))
        p = {
            'norm_mix_pre': row2(norm_mix_pre[l]), 'norm_mix_post': row2(norm_mix_post[l]),
            'norm_ffn_pre': row2(norm_ffn_pre[l]), 'norm_ffn_post': row2(norm_ffn_post[l]),
            'w_in': w_in[l].astype(BF16), 'attn_sink': attn_sink[l],
            'conv_w': conv_w[l], 'conv_b': conv_b[l],
            'filt_w1': w1.T, 'filt_b1': col2(filt_b1[l]), 'filt_freq1': col2(filt_freq1[l]),
            'filt_w2': filt_w2[l].T, 'filt_b2': col2(filt_b2[l]), 'filt_freq2': col2(filt_freq2[l]),
            'filt_w3': filt_w3[l], 'filt_b3': row2(filt_b3[l]), 'filt_decay': row2(filt_decay[l]),
            'hyena_skip': hyena_skip[l], 'w_pa': w_pa[l].astype(BF16), 'w_ph': w_ph[l].astype(BF16),
            'w_gate': w_gate[l].astype(BF16), 'b_gate': row2(b_gate[l]), 'w_o': w_o[l].astype(BF16),
            'w_up': w_up[l].astype(BF16), 'w_down': w_down[l].astype(BF16),
        }
        mod = _modulation(cond, w_mod[l], b_mod[l])
        mod_p = mod[0:1].reshape(1, 1, 6 * D)
        mod_s = mod[1:1 + Bs].reshape(Bs, 1, 6 * D)
        yp, k_ctx, v_ctx = _layer(yp, mod_p, p, Bp, Lp, False, None, None)
        new_k.append(k_ctx.reshape(Bp, Lp, N_KV_HEADS, HEAD_DIM))
        new_v.append(v_ctx.reshape(Bp, Lp, N_KV_HEADS, HEAD_DIM))
        ck = cache_k[:, l].reshape(Bs * past, KV_WIDTH)
        cv = jnp.swapaxes(cache_v[:, l].reshape(Bs, past, KV_WIDTH), 1, 2).reshape(Bs * KV_WIDTH, past)
        ys, _, _ = _layer(ys, mod_s, p, Bs, Ls, True, ck, cv)
    return (yp.reshape(Bp, Lp, D), ys.reshape(Bs, Ls, D),
            jnp.stack(new_k, axis=1), jnp.stack(new_v, axis=1))
```

```python
import functools
import math

import numpy as np
import jax
import jax.numpy as jnp
from jax import lax
from jax.experimental import pallas as pl
from jax.experimental.pallas import tpu as pltpu

D_MODEL = 1024
GRID_W = 64
N_HEADS = 8
N_KV_HEADS = 2
HEAD_DIM = 128
GROUP = N_HEADS // N_KV_HEADS
ATTN_WIDTH = N_HEADS * HEAD_DIM
KV_WIDTH = N_KV_HEADS * HEAD_DIM
WINDOW = 128
ATTN_SCALE = HEAD_DIM ** -0.5
LOG2E = math.log2(math.e)
ROPE_THETA = 10000.0
HYENA_WIDTH = D_MODEL // 2
FILTER_BANDS = 8
FILTER_FEAT = 1 + 2 * FILTER_BANDS
FILTER_FEAT_PAD = 32
FILTER_HIDDEN = 64
D_FF = -(-8 * D_MODEL // (3 * 256)) * 256
IN_WIDTH = ATTN_WIDTH + 2 * KV_WIDTH + 3 * HYENA_WIDTH
EPS = 1e-6
NEG_INF = -1e30

V7X_VMEM_BYTES = 64 * 1024 * 1024
VMEM_LIMIT = V7X_VMEM_BYTES - 8 * 1024 * 1024
MXU_COLS = 256
FFN_CHUNK = MXU_COLS
ROW_TILE = 512
ATTN_Q_TILE = 128
ATTN_KEY_SPAN = ATTN_Q_TILE + 2 * WINDOW
DFT_FINE_ROWS = 32
HYENA_CH_TILE = 256
HYENA_ROWS_PER_STEP = 2048

BF16 = jnp.bfloat16
F32 = jnp.float32


def _params(*semantics):
    return pltpu.CompilerParams(dimension_semantics=semantics, vmem_limit_bytes=VMEM_LIMIT)


def _const_spec(shape):
    zeros = (0,) * len(shape)
    return pl.BlockSpec(shape, lambda *_: zeros, pipeline_mode=pl.Buffered(1))


def _dot(a, b):
    return jnp.dot(a, b, preferred_element_type=F32)


def _rmsnorm(x, g):
    return x * lax.rsqrt(jnp.mean(x * x, axis=-1, keepdims=True) + EPS) * g


def _silu(x):
    return x * jax.nn.sigmoid(x)


def _mod_kernel(cond_ref, w_ref, b_ref, o_ref):
    a = _silu(cond_ref[...]).astype(BF16)
    o_ref[...] = _dot(a, w_ref[...].astype(BF16)) + b_ref[...]


def _modulation(cond, w_mod, b_mod):
    rows = cond.shape[0]
    n = w_mod.shape[1]
    tn = D_MODEL
    return pl.pallas_call(
        _mod_kernel,
        grid=(n // tn,),
        in_specs=[pl.BlockSpec((rows, D_MODEL), lambda j: (0, 0)),
                  pl.BlockSpec((D_MODEL, tn), lambda j: (0, j)),
                  pl.BlockSpec((1, tn), lambda j: (0, j))],
        out_specs=pl.BlockSpec((rows, tn), lambda j: (0, j)),
        out_shape=jax.ShapeDtypeStruct((rows, n), F32),
        compiler_params=_params("arbitrary"),
        name="modulation",
    )(cond, w_mod, b_mod.reshape(1, n))


def _dft_seed_tables(L):
    bs = DFT_FINE_ROWS
    t = np.arange(L, dtype=np.int64)
    theta = math.pi / L
    ang_a = ((np.arange(L // bs, dtype=np.int64)[:, None] * bs * t[None, :]) % (2 * L)) * theta
    ang_b = ((np.arange(bs, dtype=np.int64)[:, None] * t[None, :]) % (2 * L)) * theta
    return [jnp.asarray(f(a), F32) for a in (ang_a, ang_b) for f in (np.cos, np.sin)]


def _spectrum_kernel(feat_ref, w1_ref, b1_ref, f1_ref, w2_ref, b2_ref, f2_ref, w3_ref, b3_ref,
                     decay_ref, tn_ref, ca_ref, sa_ref, cb_ref, sb_ref,
                     hr_ref, hi_ref, hn_ref, cos_ref, sin_ref, a_s, b_s, *, L, tk):
    i = pl.program_id(0)
    hi_prec = lax.Precision.HIGHEST

    bs = DFT_FINE_ROWS
    cb, sb = cb_ref[...], sb_ref[...]
    for j in range(tk // bs):
        ca = ca_ref[pl.ds(i * (tk // bs) + j, 1), :]
        sa = sa_ref[pl.ds(i * (tk // bs) + j, 1), :]
        cos_ref[j * bs:(j + 1) * bs, :] = (ca * cb - sa * sb).astype(BF16)
        sin_ref[j * bs:(j + 1) * bs, :] = (sa * cb + ca * sb).astype(BF16)

    @pl.when(i == 0)
    def _():
        h = jnp.dot(w1_ref[...], feat_ref[...], precision=hi_prec, preferred_element_type=F32)
        h = jnp.sin(f1_ref[...] * (h + b1_ref[...]))
        h = jnp.dot(w2_ref[...], h, precision=hi_prec, preferred_element_type=F32)
        h = jnp.sin(f2_ref[...] * (h + b2_ref[...]))
        h = _dot(h.T.astype(BF16), w3_ref[...].astype(BF16)) + b3_ref[...]
        h = h * jnp.exp(-tn_ref[...] * jnp.abs(decay_ref[...]))
        fwd = h[:, :HYENA_WIDTH]
        bwd = h[:, HYENA_WIDTH:]
        row = lax.broadcasted_iota(jnp.int32, (L, HYENA_WIDTH), 0)
        bwd = jnp.where(row == 0, 0.0, bwd)
        even = fwd + bwd
        a_s[...] = even.astype(BF16)
        b_s[...] = (fwd - bwd).astype(BF16)
        sign = jnp.where((row & 1) == 0, 1.0, -1.0)
        hn_ref[...] = jnp.sum(even * sign, axis=0, keepdims=True) * (1.0 / (2 * L))

    krow = lax.broadcasted_iota(jnp.int32, (tk, HYENA_WIDTH), 0) + i * tk
    wk = jnp.where(krow == 0, 1.0, 2.0) * (1.0 / (2 * L))
    hr_ref[...] = _dot(cos_ref[...], a_s[...]) * wk
    hi_ref[...] = _dot(sin_ref[...], b_s[...]) * (-wk)


def _spectrum(L, feat, tnorm, w1, b1, f1, w2, b2, f2, w3, b3, decay):
    tk = min(L, 512)
    C = HYENA_WIDTH
    small = [feat, w1, b1, f1, w2, b2, f2, w3, b3, decay, tnorm] + _dft_seed_tables(L)
    blk = lambda w: pl.BlockSpec((tk, w), lambda i: (i, 0))
    return pl.pallas_call(
        functools.partial(_spectrum_kernel, L=L, tk=tk),
        grid=(L // tk,),
        in_specs=[_const_spec(a.shape) for a in small],
        out_specs=[blk(C), blk(C), pl.BlockSpec((1, C), lambda i: (0, 0)), blk(L), blk(L)],
        out_shape=[jax.ShapeDtypeStruct((L, C), F32), jax.ShapeDtypeStruct((L, C), F32),
                   jax.ShapeDtypeStruct((1, C), F32),
                   jax.ShapeDtypeStruct((L, L), BF16), jax.ShapeDtypeStruct((L, L), BF16)],
        scratch_shapes=[pltpu.VMEM((L, C), BF16), pltpu.VMEM((L, C), BF16)],
        compiler_params=_params("arbitrary"),
        name=f"spectrum_{L}",
    )(*small)


def _modulated_norm(x, g, shift, scale):
    return _rmsnorm(x, g) * (1.0 + scale) + shift


def _rope(x, cos, sin_signed):
    lane = lax.broadcasted_iota(jnp.int32, x.shape, 1)
    swapped = jnp.where((lane & 63) < 32,
                        pltpu.roll(x, HEAD_DIM - 32, axis=1),
                        pltpu.roll(x, 32, axis=1))
    return x * cos + swapped * sin_signed


def _inproj_kernel(*refs, rope):
    if rope:
        x_ref, mod_ref, g_ref, w_ref, cos_ref, sin_ref, q_ref, k_ref, v_ref, vt_ref, hy_ref = refs
    else:
        x_ref, mod_ref, g_ref, w_ref, q_ref, k_ref, v_ref, vt_ref, hy_ref = refs
    m = mod_ref[0]
    h = _modulated_norm(x_ref[...], g_ref[...], m[:, 0:D_MODEL], m[:, D_MODEL:2 * D_MODEL]).astype(BF16)

    def heads(y, out_ref, n, scale):
        if not rope:
            out_ref[...] = (y if scale == 1.0 else y * scale).astype(out_ref.dtype)
            return
        cos, sin = cos_ref[...], sin_ref[...]
        if scale != 1.0:
            cos, sin = cos * scale, sin * scale
        for hd in range(n):
            cols = slice(hd * HEAD_DIM, (hd + 1) * HEAD_DIM)
            out_ref[:, cols] = _rope(y[:, cols], cos, sin).astype(out_ref.dtype)

    heads(_dot(h, w_ref[:, :ATTN_WIDTH]), q_ref, N_HEADS, ATTN_SCALE * LOG2E)
    kv = _dot(h, w_ref[:, ATTN_WIDTH:ATTN_WIDTH + 2 * KV_WIDTH])
    heads(kv[:, :KV_WIDTH], k_ref, N_KV_HEADS, 1.0)
    v = kv[:, KV_WIDTH:]
    v_ref[...] = v.astype(v_ref.dtype)
    vt_ref[...] = v.T.astype(vt_ref.dtype)
    hy_ref[...] = _dot(h, w_ref[:, ATTN_WIDTH + 2 * KV_WIDTH:]).astype(hy_ref.dtype)


def _inproj(x, mod, g_pre, w_in, L, rope_tabs, kv_dtype):
    R = x.shape[0]
    tm = ROW_TILE
    steps_per_mod = (R // mod.shape[0]) // tm
    row = lambda i: (i, 0)
    in_specs = [pl.BlockSpec((tm, D_MODEL), row),
                pl.BlockSpec((1, 1, mod.shape[2]), lambda i: (i // steps_per_mod, 0, 0)),
                _const_spec(g_pre.shape), _const_spec(w_in.shape)]
    args = [x, mod, g_pre, w_in]
    if rope_tabs is not None:
        steps_per_seq = L // tm
        tab = pl.BlockSpec((tm, HEAD_DIM), lambda i: (i % steps_per_seq, 0))
        in_specs += [tab, tab]
        args += list(rope_tabs)
    return pl.pallas_call(
        functools.partial(_inproj_kernel, rope=rope_tabs is not None),
        grid=(R // tm,),
        in_specs=in_specs,
        out_specs=[pl.BlockSpec((tm, ATTN_WIDTH), row), pl.BlockSpec((tm, KV_WIDTH), row),
                   pl.BlockSpec((tm, KV_WIDTH), row), pl.BlockSpec((KV_WIDTH, tm), lambda i: (0, i)),
                   pl.BlockSpec((tm, 3 * HYENA_WIDTH), row)],
        out_shape=[jax.ShapeDtypeStruct((R, ATTN_WIDTH), BF16), jax.ShapeDtypeStruct((R, KV_WIDTH), kv_dtype),
                   jax.ShapeDtypeStruct((R, KV_WIDTH), kv_dtype), jax.ShapeDtypeStruct((KV_WIDTH, R), BF16),
                   jax.ShapeDtypeStruct((R, 3 * HYENA_WIDTH), BF16)],
        compiler_params=_params("arbitrary"),
        name=f"inproj_{L}",
    )(*args)


def _rope_tables(L):
    rows = L // GRID_W
    row_ids = jnp.repeat(jnp.arange(rows), GRID_W)
    col_ids = jnp.tile(jnp.arange(GRID_W), rows)
    half = HEAD_DIM // 2
    inv_freq = ROPE_THETA ** (-jnp.arange(0, half, 2, dtype=F32) / half)
    cos_parts, sin_parts = [], []
    for pos in (row_ids, col_ids):
        ang = pos.astype(F32)[:, None] * inv_freq[None, :]
        cos_parts += [jnp.cos(ang), jnp.cos(ang)]
        sin_parts += [-jnp.sin(ang), jnp.sin(ang)]
    return jnp.concatenate(cos_parts, axis=-1), jnp.concatenate(sin_parts, axis=-1)


def _band_bias():
    key = np.arange(ATTN_KEY_SPAN)[:, None]
    qry = (np.arange(GROUP * ATTN_Q_TILE) % ATTN_Q_TILE)[None, :]
    tabs = [np.where(np.abs(key - off - qry) <= WINDOW, 0.0, NEG_INF) for off in (0, WINDOW, 2 * WINDOW)]
    return jnp.asarray(np.stack(tabs), F32)


def _attn_kernel(*refs, windowed, L, nblk):
    if windowed:
        sink_ref, q_ref, k_ref, vt_ref, ck_ref, cvt_ref, bias_ref, o_ref = refs
    else:
        sink_ref, q_ref, k_ref, vt_ref, o_ref = refs
    tq = ATTN_Q_TILE
    nt = (((1,), (1,)), ((), ()))
    for jb in range(nblk):
        rows = slice(jb * tq, (jb + 1) * tq)
        if windowed:
            qs = (pl.program_id(1) * nblk + jb) * tq
            start = pl.multiple_of(jnp.clip(qs - WINDOW, 0, L - ATTN_KEY_SPAN), WINDOW)
            bias = bias_ref[(qs - start) // WINDOW]
        for kk in range(N_KV_HEADS):
            heads = [kk * GROUP + g for g in range(GROUP)]
            kv = slice(kk * HEAD_DIM, (kk + 1) * HEAD_DIM)
            q4 = jnp.concatenate([q_ref[rows, hd * HEAD_DIM:(hd + 1) * HEAD_DIM] for hd in heads], axis=0)
            sink = jnp.concatenate([jnp.full((1, tq), sink_ref[hd] * LOG2E, F32) for hd in heads], axis=1)
            if windowed:
                s_w = lax.dot_general(k_ref[pl.ds(start, ATTN_KEY_SPAN), kv], q4, nt,
                                      preferred_element_type=F32) + bias
                s_c = lax.dot_general(ck_ref[:, kv].astype(BF16), q4, nt, preferred_element_type=F32)
                m = jnp.maximum(jnp.maximum(jnp.max(s_w, axis=0, keepdims=True),
                                            jnp.max(s_c, axis=0, keepdims=True)), sink)
                p_w = jnp.exp2(s_w - m)
                p_c = jnp.exp2(s_c - m)
                denom = (jnp.sum(p_w, axis=0, keepdims=True) + jnp.sum(p_c, axis=0, keepdims=True)
                         + jnp.exp2(sink - m))
                o = (_dot(vt_ref[kv, pl.ds(start, ATTN_KEY_SPAN)], p_w.astype(BF16))
                     + _dot(cvt_ref[kv, :].astype(BF16), p_c.astype(BF16)))
            else:
                s = lax.dot_general(k_ref[:, kv].astype(BF16), q4, nt, preferred_element_type=F32)
                m = jnp.maximum(jnp.max(s, axis=0, keepdims=True), sink)
                p = jnp.exp2(s - m)
                denom = jnp.sum(p, axis=0, keepdims=True) + jnp.exp2(sink - m)
                o = _dot(vt_ref[kv, :], p.astype(BF16))
            o = (o * (1.0 / denom)).T
            for g, hd in enumerate(heads):
                o_ref[rows, hd * HEAD_DIM:(hd + 1) * HEAD_DIM] = o[g * tq:(g + 1) * tq].astype(o_ref.dtype)


def _attention(q, k, vt, sink, B, L, ctx_k=None, ctx_vt=None):
    windowed = ctx_k is not None
    nblk = min(L // ATTN_Q_TILE, 4)
    tstep = nblk * ATTN_Q_TILE
    nq = L // tstep
    sink_spec = pl.BlockSpec(memory_space=pltpu.SMEM)
    q_spec = pl.BlockSpec((tstep, ATTN_WIDTH), lambda b, i: (b * nq + i, 0))
    in_specs = [sink_spec, q_spec,
                pl.BlockSpec((L, KV_WIDTH), lambda b, i: (b, 0)),
                pl.BlockSpec((KV_WIDTH, L), lambda b, i: (0, b))]
    args = [sink, q, k, vt]
    if windowed:
        assert L >= ATTN_KEY_SPAN + WINDOW
        P = ctx_k.shape[0] // B
        bias = _band_bias()
        in_specs += [pl.BlockSpec((P, KV_WIDTH), lambda b, i: (b, 0)),
                     pl.BlockSpec((KV_WIDTH, P), lambda b, i: (b, 0)),
                     _const_spec(bias.shape)]
        args += [ctx_k, ctx_vt, bias]
    return pl.pallas_call(
        functools.partial(_attn_kernel, windowed=windowed, L=L, nblk=nblk),
        grid=(B, nq),
        in_specs=in_specs,
        out_specs=q_spec,
        out_shape=jax.ShapeDtypeStruct((B * L, ATTN_WIDTH), BF16),
        compiler_params=_params("arbitrary", "arbitrary"),
        name=f"attention_{L}",
    )(*args)


def _hyena_kernel(x0_ref, x1_ref, v_ref, cw0_ref, cw1_ref, cwv_ref, cb0_ref, cb1_ref, cbv_ref,
                  hr_ref, hi_ref, hn_ref, skip_ref, cos_ref, sin_ref, o_ref, *, L, nseq):
    rows, tc = o_ref.shape
    pos = lax.broadcasted_iota(jnp.int32, (rows, tc), 0) & (L - 1)
    first = pos == 0
    last = pos == L - 1

    def short_conv(u_ref, w_ref, b_ref):
        u = u_ref[...].astype(F32)
        prev = jnp.where(first, 0.0, pltpu.roll(u, 1, axis=0))
        nxt = jnp.where(last, 0.0, pltpu.roll(u, rows - 1, axis=0))
        return prev * w_ref[0:1, :] + u * w_ref[1:2, :] + nxt * w_ref[2:3, :] + b_ref[...]

    def by_seq(a):
        return a if nseq == 1 else jnp.concatenate([a[s * L:(s + 1) * L] for s in range(nseq)], axis=1)

    def by_row(a):
        return a if nseq == 1 else jnp.concatenate([a[:, s * tc:(s + 1) * tc] for s in range(nseq)], axis=0)

    def tiled(a):
        return a if nseq == 1 else jnp.concatenate([a] * nseq, axis=1)

    z = by_seq(short_conv(v_ref, cwv_ref, cbv_ref) * short_conv(x1_ref, cw1_ref, cb1_ref))
    zb = z.astype(BF16)
    sign = jnp.where((lax.broadcasted_iota(jnp.int32, z.shape, 0) & 1) == 0, 1.0, -1.0)
    z_nyq = jnp.sum(z * sign, axis=0, keepdims=True)
    zc = _dot(cos_ref[...], zb)
    zs = _dot(sin_ref[...], zb)
    hr = tiled(hr_ref[...])
    hi = tiled(hi_ref[...])
    wr = (zc * hr + zs * hi).astype(BF16)
    wi = (zc * hi - zs * hr).astype(BF16)
    y = _dot(cos_ref[...], wr) - _dot(sin_ref[...], wi)
    y = y + sign * (z_nyq * tiled(hn_ref[...])) + z * tiled(skip_ref[...])
    o_ref[...] = (short_conv(x0_ref, cw0_ref, cb0_ref) * by_row(y)).astype(o_ref.dtype)


def _hyena(hy, conv_w, conv_b, hr, hi, hn, skip, cos_t, sin_t, B, L):
    C = HYENA_WIDTH
    tc = HYENA_CH_TILE
    nj = C // tc
    assert L & (L - 1) == 0
    nseq = min(B, max(1, HYENA_ROWS_PER_STEP // L))

    def part(p, rows):
        return pl.BlockSpec((rows, tc), lambda b, j: (b if rows == nseq * L else 0, p * nj + j))

    spec_c = pl.BlockSpec((L, tc), lambda b, j: (0, j))
    vec_c = pl.BlockSpec((1, tc), lambda b, j: (0, j))
    conv_b = conv_b.reshape(1, 3 * C)
    seq = nseq * L
    return pl.pallas_call(
        functools.partial(_hyena_kernel, L=L, nseq=nseq),
        grid=(B // nseq, nj),
        in_specs=[part(0, seq), part(1, seq), part(2, seq),
                  part(0, 3), part(1, 3), part(2, 3),
                  part(0, 1), part(1, 1), part(2, 1),
                  spec_c, spec_c, vec_c, vec_c,
                  _const_spec((L, L)), _const_spec((L, L))],
        out_specs=pl.BlockSpec((seq, tc), lambda b, j: (b, j)),
        out_shape=jax.ShapeDtypeStruct((B * L, C), BF16),
        compiler_params=_params("arbitrary", "arbitrary"),
        name=f"hyena_{L}",
    )(hy, hy, hy, conv_w, conv_w, conv_w, conv_b, conv_b, conv_b,
      hr, hi, hn, skip.reshape(1, C), cos_t, sin_t)


def _merge_ffn_kernel(x_ref, attn_ref, hyo_ref, mod_ref, g_pre_ref, g_post_ref, g_fpre_ref, g_fpost_ref,
                      wg_ref, bg_ref, wpa_ref, wph_ref, wo_ref, wup_ref, wdn_ref, o_ref):
    D = D_MODEL
    m = mod_ref[0]
    sh1, sc1, g1, sh2, sc2, g2 = [m[:, j * D:(j + 1) * D] for j in range(6)]
    x = x_ref[...]
    h = _modulated_norm(x, g_pre_ref[...], sh1, sc1).astype(BF16)
    gate_a = jax.nn.sigmoid(_dot(h, wg_ref[:, :D]) + bg_ref[:, :D])
    merged = gate_a * _dot(attn_ref[...], wpa_ref[...])
    gate_h = jax.nn.sigmoid(_dot(h, wg_ref[:, D:]) + bg_ref[:, D:])
    merged = merged + gate_h * _dot(hyo_ref[...], wph_ref[...])
    mix = _dot(merged.astype(BF16), wo_ref[...])
    x = x + g1 * _rmsnorm(mix, g_post_ref[...])
    h2 = _modulated_norm(x, g_fpre_ref[...], sh2, sc2).astype(BF16)
    f = jnp.zeros(x.shape, F32)
    for c0 in range(0, D_FF, FFN_CHUNK):
        gt = _dot(h2, wup_ref[:, c0:c0 + FFN_CHUNK])
        up = _dot(h2, wup_ref[:, D_FF + c0:D_FF + c0 + FFN_CHUNK])
        f = f + _dot((_silu(gt) * up).astype(BF16), wdn_ref[c0:c0 + FFN_CHUNK, :])
    o_ref[...] = x + g2 * _rmsnorm(f, g_fpost_ref[...])


def _merge_ffn(x, attn, hyo, mod, norms, w_gate, b_gate, w_pa, w_ph, w_o, w_up, w_down, L):
    R = x.shape[0]
    tm = ROW_TILE
    steps_per_mod = (R // mod.shape[0]) // tm
    row = lambda i: (i, 0)
    consts = list(norms) + [w_gate, b_gate, w_pa, w_ph, w_o, w_up, w_down]
    return pl.pallas_call(
        _merge_ffn_kernel,
        grid=(R // tm,),
        in_specs=[pl.BlockSpec((tm, D_MODEL), row), pl.BlockSpec((tm, ATTN_WIDTH), row),
                  pl.BlockSpec((tm, HYENA_WIDTH), row),
                  pl.BlockSpec((1, 1, mod.shape[2]), lambda i: (i // steps_per_mod, 0, 0))]
                 + [_const_spec(a.shape) for a in consts],
        out_specs=pl.BlockSpec((tm, D_MODEL), row),
        out_shape=jax.ShapeDtypeStruct((R, D_MODEL), F32),
        compiler_params=_params("arbitrary"),
        name=f"merge_ffn_{L}",
    )(x, attn, hyo, mod, *consts)


def _filter_features(L):
    t = jnp.arange(L, dtype=F32)
    t_norm = t / L
    bands = jnp.linspace(1e-4, FILTER_BANDS - 1, FILTER_BANDS).astype(F32)
    ang = (2.0 * math.pi * t / L)[:, None] * bands[None, :]
    feat = jnp.concatenate([t_norm[:, None], jnp.cos(ang), jnp.sin(ang)], axis=-1)
    feat = jnp.pad(feat, ((0, 0), (0, FILTER_FEAT_PAD - FILTER_FEAT)))
    return feat.T, t_norm[:, None]


def _layer(x, mod, p, B, L, latent, ctx_k, ctx_v):
    feat, tnorm = _filter_features(L)
    hr, hi, hn, cos_t, sin_t = _spectrum(L, feat, tnorm, p['filt_w1'], p['filt_b1'], p['filt_freq1'],
                                         p['filt_w2'], p['filt_b2'], p['filt_freq2'], p['filt_w3'],
                                         p['filt_b3'], p['filt_decay'])
    rope_tabs = _rope_tables(L) if latent else None
    q, k, v, vt, hy = _inproj(x, mod, p['norm_mix_pre'], p['w_in'], L, rope_tabs, BF16 if latent else F32)
    attn = _attention(q, k, vt, p['attn_sink'], B, L, ctx_k, ctx_v)
    hyo = _hyena(hy, p['conv_w'], p['conv_b'], hr, hi, hn, p['hyena_skip'], cos_t, sin_t, B, L)
    norms = [p['norm_mix_pre'], p['norm_mix_post'], p['norm_ffn_pre'], p['norm_ffn_post']]
    y = _merge_ffn(x, attn, hyo, mod, norms, p['w_gate'], p['b_gate'], p['w_pa'], p['w_ph'], p['w_o'],
                   p['w_up'], p['w_down'], L)
    return y, k, v


def kernel(x_prompt, x_sample, c, cache_k, cache_v, c_ctx, norm_mix_pre, norm_mix_post, norm_ffn_pre, norm_ffn_post, w_mod, b_mod, w_in, attn_sink, conv_w, conv_b, filt_w1, filt_b1, filt_freq1, filt_w2, filt_b2, filt_freq2, filt_w3, filt_b3, filt_decay, hyena_skip, w_pa, w_ph, w_gate, b_gate, w_o, w_up, w_down):
    Bp, Lp, D = x_prompt.shape
    Bs, Ls, _ = x_sample.shape
    depth = w_in.shape[0]
    past = cache_k.shape[2]
    yp = x_prompt.reshape(Bp * Lp, D)
    ys = x_sample.reshape(Bs * Ls, D)
    cond = jnp.concatenate([c_ctx[None, :], c, jnp.zeros((8 - 1 - Bs, D), F32)], axis=0)
    new_k, new_v = [], []
    row2 = lambda a: a.reshape(1, -1)
    col2 = lambda a: a.reshape(-1, 1)
    for l in range(depth):
        w1 = jnp.pad(filt_w1[l], ((0, FILTER_FEAT_PAD - FILTER_FEAT), (0, 0)))
        p = {
            'norm_mix_pre': row2(norm_mix_pre[l]), 'norm_mix_post': row2(norm_mix_post[l]),
            'norm_ffn_pre': row2(norm_ffn_pre[l]), 'norm_ffn_post': row2(norm_ffn_post[l]),
            'w_in': w_in[l].astype(BF16), 'attn_sink': attn_sink[l],
            'conv_w': conv_w[l], 'conv_b': conv_b[l],
            'filt_w1': w1.T, 'filt_b1': col2(filt_b1[l]), 'filt_freq1': col2(filt_freq1[l]),
            'filt_w2': filt_w2[l].T, 'filt_b2': col2(filt_b2[l]), 'filt_freq2': col2(filt_freq2[l]),
            'filt_w3': filt_w3[l], 'filt_b3': row2(filt_b3[l]), 'filt_decay': row2(filt_decay[l]),
            'hyena_skip': hyena_skip[l], 'w_pa': w_pa[l].astype(BF16), 'w_ph': w_ph[l].astype(BF16),
            'w_gate': w_gate[l].astype(BF16), 'b_gate': row2(b_gate[l]), 'w_o': w_o[l].astype(BF16),
            'w_up': w_up[l].astype(BF16), 'w_down': w_down[l].astype(BF16),
        }
        mod = _modulation(cond, w_mod[l], b_mod[l])
        mod_p = mod[0:1].reshape(1, 1, 6 * D)
        mod_s = mod[1:1 + Bs].reshape(Bs, 1, 6 * D)
        yp, k_ctx, v_ctx = _layer(yp, mod_p, p, Bp, Lp, False, None, None)
        new_k.append(k_ctx.reshape(Bp, Lp, N_KV_HEADS, HEAD_DIM))
        new_v.append(v_ctx.reshape(Bp, Lp, N_KV_HEADS, HEAD_DIM))
        ck = cache_k[:, l].reshape(Bs * past, KV_WIDTH)
        cv = jnp.swapaxes(cache_v[:, l].reshape(Bs, past, KV_WIDTH), 1, 2).reshape(Bs * KV_WIDTH, past)
        ys, _, _ = _layer(ys, mod_s, p, Bs, Ls, True, ck, cv)
    return (yp.reshape(Bp, Lp, D), ys.reshape(Bs, Ls, D),
            jnp.stack(new_k, axis=1), jnp.stack(new_v, axis=1))
```

```python
import functools
import math

import numpy as np
import jax
import jax.numpy as jnp
from jax import lax
from jax.experimental import pallas as pl
from jax.experimental.pallas import tpu as pltpu

D_MODEL = 1024
GRID_W = 64
N_HEADS = 8
N_KV_HEADS = 2
HEAD_DIM = 128
GROUP = N_HEADS // N_KV_HEADS
ATTN_WIDTH = N_HEADS * HEAD_DIM
KV_WIDTH = N_KV_HEADS * HEAD_DIM
WINDOW = 128
ATTN_SCALE = HEAD_DIM ** -0.5
LOG2E = math.log2(math.e)
ROPE_THETA = 10000.0
HYENA_WIDTH = D_MODEL // 2
FILTER_BANDS = 8
FILTER_FEAT = 1 + 2 * FILTER_BANDS
FILTER_FEAT_PAD = 32
FILTER_HIDDEN = 64
D_FF = -(-8 * D_MODEL // (3 * 256)) * 256
IN_WIDTH = ATTN_WIDTH + 2 * KV_WIDTH + 3 * HYENA_WIDTH
EPS = 1e-6
NEG_INF = -1e30

V7X_VMEM_BYTES = 64 * 1024 * 1024
VMEM_LIMIT = V7X_VMEM_BYTES - 8 * 1024 * 1024
MXU_COLS = 256
FFN_CHUNK = MXU_COLS
ROW_TILE = 512
ATTN_KEY_CHUNK = 64
ATTN_Q_TILE = 128
ATTN_KEY_SPAN = ATTN_Q_TILE + 2 * WINDOW
DFT_FINE_ROWS = 32
HYENA_CH_TILE = 256
HYENA_ROWS_PER_STEP = 2048

BF16 = jnp.bfloat16
F32 = jnp.float32


def _params(*semantics):
    return pltpu.CompilerParams(dimension_semantics=semantics, vmem_limit_bytes=VMEM_LIMIT)


def _const_spec(shape):
    zeros = (0,) * len(shape)
    return pl.BlockSpec(shape, lambda *_: zeros, pipeline_mode=pl.Buffered(1))


def _dot(a, b):
    return jnp.dot(a, b, preferred_element_type=F32)


def _rmsnorm(x, g):
    return x * lax.rsqrt(jnp.mean(x * x, axis=-1, keepdims=True) + EPS) * g


def _silu(x):
    return x * jax.nn.sigmoid(x)


def _mod_kernel(cond_ref, w_ref, b_ref, o_ref):
    a = _silu(cond_ref[...]).astype(BF16)
    o_ref[...] = _dot(a, w_ref[...].astype(BF16)) + b_ref[...]


def _modulation(cond, w_mod, b_mod):
    rows = cond.shape[0]
    n = w_mod.shape[1]
    tn = D_MODEL
    return pl.pallas_call(
        _mod_kernel,
        grid=(n // tn,),
        in_specs=[pl.BlockSpec((rows, D_MODEL), lambda j: (0, 0)),
                  pl.BlockSpec((D_MODEL, tn), lambda j: (0, j)),
                  pl.BlockSpec((1, tn), lambda j: (0, j))],
        out_specs=pl.BlockSpec((rows, tn), lambda j: (0, j)),
        out_shape=jax.ShapeDtypeStruct((rows, n), F32),
        compiler_params=_params("arbitrary"),
        name="modulation",
    )(cond, w_mod, b_mod.reshape(1, n))


def _dft_seed_tables(L):
    bs = DFT_FINE_ROWS
    t = np.arange(L, dtype=np.int64)
    theta = math.pi / L
    ang_a = ((np.arange(L // bs, dtype=np.int64)[:, None] * bs * t[None, :]) % (2 * L)) * theta
    ang_b = ((np.arange(bs, dtype=np.int64)[:, None] * t[None, :]) % (2 * L)) * theta
    return [jnp.asarray(f(a), F32) for a in (ang_a, ang_b) for f in (np.cos, np.sin)]


def _spectrum_kernel(feat_ref, w1_ref, b1_ref, f1_ref, w2_ref, b2_ref, f2_ref, w3_ref, b3_ref,
                     decay_ref, tn_ref, ca_ref, sa_ref, cb_ref, sb_ref,
                     hr_ref, hi_ref, hn_ref, cos_ref, sin_ref, a_s, b_s, *, L, tk):
    i = pl.program_id(0)
    hi_prec = lax.Precision.HIGHEST

    bs = DFT_FINE_ROWS
    cb, sb = cb_ref[...], sb_ref[...]
    for j in range(tk // bs):
        ca = ca_ref[pl.ds(i * (tk // bs) + j, 1), :]
        sa = sa_ref[pl.ds(i * (tk // bs) + j, 1), :]
        cos_ref[j * bs:(j + 1) * bs, :] = (ca * cb - sa * sb).astype(BF16)
        sin_ref[j * bs:(j + 1) * bs, :] = (sa * cb + ca * sb).astype(BF16)

    @pl.when(i == 0)
    def _():
        h = jnp.dot(w1_ref[...], feat_ref[...], precision=hi_prec, preferred_element_type=F32)
        h = jnp.sin(f1_ref[...] * (h + b1_ref[...]))
        h = jnp.dot(w2_ref[...], h, precision=hi_prec, preferred_element_type=F32)
        h = jnp.sin(f2_ref[...] * (h + b2_ref[...]))
        h = _dot(h.T.astype(BF16), w3_ref[...].astype(BF16)) + b3_ref[...]
        h = h * jnp.exp(-tn_ref[...] * jnp.abs(decay_ref[...]))
        fwd = h[:, :HYENA_WIDTH]
        bwd = h[:, HYENA_WIDTH:]
        row = lax.broadcasted_iota(jnp.int32, (L, HYENA_WIDTH), 0)
        bwd = jnp.where(row == 0, 0.0, bwd)
        even = fwd + bwd
        a_s[...] = even.astype(BF16)
        b_s[...] = (fwd - bwd).astype(BF16)
        sign = jnp.where((row & 1) == 0, 1.0, -1.0)
        hn_ref[...] = jnp.sum(even * sign, axis=0, keepdims=True) * (1.0 / (2 * L))

    krow = lax.broadcasted_iota(jnp.int32, (tk, HYENA_WIDTH), 0) + i * tk
    wk = jnp.where(krow == 0, 1.0, 2.0) * (1.0 / (2 * L))
    hr_ref[...] = _dot(cos_ref[...], a_s[...]) * wk
    hi_ref[...] = _dot(sin_ref[...], b_s[...]) * (-wk)


def _spectrum(L, feat, tnorm, w1, b1, f1, w2, b2, f2, w3, b3, decay):
    tk = min(L, 512)
    C = HYENA_WIDTH
    small = [feat, w1, b1, f1, w2, b2, f2, w3, b3, decay, tnorm] + _dft_seed_tables(L)
    blk = lambda w: pl.BlockSpec((tk, w), lambda i: (i, 0))
    return pl.pallas_call(
        functools.partial(_spectrum_kernel, L=L, tk=tk),
        grid=(L // tk,),
        in_specs=[_const_spec(a.shape) for a in small],
        out_specs=[blk(C), blk(C), pl.BlockSpec((1, C), lambda i: (0, 0)), blk(L), blk(L)],
        out_shape=[jax.ShapeDtypeStruct((L, C), F32), jax.ShapeDtypeStruct((L, C), F32),
                   jax.ShapeDtypeStruct((1, C), F32),
                   jax.ShapeDtypeStruct((L, L), BF16), jax.ShapeDtypeStruct((L, L), BF16)],
        scratch_shapes=[pltpu.VMEM((L, C), BF16), pltpu.VMEM((L, C), BF16)],
        compiler_params=_params("arbitrary"),
        name=f"spectrum_{L}",
    )(*small)


def _modulated_norm(x, g, shift, scale):
    return _rmsnorm(x, g) * (1.0 + scale) + shift


def _rope(x, cos, sin_signed):
    lane = lax.broadcasted_iota(jnp.int32, x.shape, 1)
    swapped = jnp.where((lane & 63) < 32,
                        pltpu.roll(x, HEAD_DIM - 32, axis=1),
                        pltpu.roll(x, 32, axis=1))
    return x * cos + swapped * sin_signed


def _inproj_kernel(*refs, rope):
    if rope:
        x_ref, mod_ref, g_ref, w_ref, cos_ref, sin_ref, q_ref, k_ref, v_ref, vt_ref, hy_ref = refs
    else:
        x_ref, mod_ref, g_ref, w_ref, q_ref, k_ref, v_ref, vt_ref, hy_ref = refs
    m = mod_ref[0]
    h = _modulated_norm(x_ref[...], g_ref[...], m[:, 0:D_MODEL], m[:, D_MODEL:2 * D_MODEL]).astype(BF16)

    def heads(y, out_ref, n, scale):
        if not rope:
            out_ref[...] = (y if scale == 1.0 else y * scale).astype(out_ref.dtype)
            return
        cos, sin = cos_ref[...], sin_ref[...]
        if scale != 1.0:
            cos, sin = cos * scale, sin * scale
        for hd in range(n):
            cols = slice(hd * HEAD_DIM, (hd + 1) * HEAD_DIM)
            out_ref[:, cols] = _rope(y[:, cols], cos, sin).astype(out_ref.dtype)

    heads(_dot(h, w_ref[:, :ATTN_WIDTH]), q_ref, N_HEADS, ATTN_SCALE * LOG2E)
    kv = _dot(h, w_ref[:, ATTN_WIDTH:ATTN_WIDTH + 2 * KV_WIDTH])
    heads(kv[:, :KV_WIDTH], k_ref, N_KV_HEADS, 1.0)
    v = kv[:, KV_WIDTH:]
    v_ref[...] = v.astype(v_ref.dtype)
    vt_ref[...] = v.T.astype(vt_ref.dtype)
    hy_ref[...] = _dot(h, w_ref[:, ATTN_WIDTH + 2 * KV_WIDTH:]).astype(hy_ref.dtype)


def _inproj(x, mod, g_pre, w_in, L, rope_tabs, kv_dtype):
    R = x.shape[0]
    tm = ROW_TILE
    steps_per_mod = (R // mod.shape[0]) // tm
    row = lambda i: (i, 0)
    in_specs = [pl.BlockSpec((tm, D_MODEL), row),
                pl.BlockSpec((1, 1, mod.shape[2]), lambda i: (i // steps_per_mod, 0, 0)),
                _const_spec(g_pre.shape), _const_spec(w_in.shape)]
    args = [x, mod, g_pre, w_in]
    if rope_tabs is not None:
        steps_per_seq = L // tm
        tab = pl.BlockSpec((tm, HEAD_DIM), lambda i: (i % steps_per_seq, 0))
        in_specs += [tab, tab]
        args += list(rope_tabs)
    return pl.pallas_call(
        functools.partial(_inproj_kernel, rope=rope_tabs is not None),
        grid=(R // tm,),
        in_specs=in_specs,
        out_specs=[pl.BlockSpec((tm, ATTN_WIDTH), row), pl.BlockSpec((tm, KV_WIDTH), row),
                   pl.BlockSpec((tm, KV_WIDTH), row), pl.BlockSpec((KV_WIDTH, tm), lambda i: (0, i)),
                   pl.BlockSpec((tm, 3 * HYENA_WIDTH), row)],
        out_shape=[jax.ShapeDtypeStruct((R, ATTN_WIDTH), BF16), jax.ShapeDtypeStruct((R, KV_WIDTH), kv_dtype),
                   jax.ShapeDtypeStruct((R, KV_WIDTH), kv_dtype), jax.ShapeDtypeStruct((KV_WIDTH, R), BF16),
                   jax.ShapeDtypeStruct((R, 3 * HYENA_WIDTH), BF16)],
        compiler_params=_params("arbitrary"),
        name=f"inproj_{L}",
    )(*args)


def _rope_tables(L):
    rows = L // GRID_W
    row_ids = np.repeat(np.arange(rows), GRID_W)
    col_ids = np.tile(np.arange(GRID_W), rows)
    half = HEAD_DIM // 2
    inv_freq = ROPE_THETA ** (-np.arange(0, half, 2, dtype=np.float64) / half)
    cos_parts, sin_parts = [], []
    for pos in (row_ids, col_ids):
        ang = pos.astype(np.float64)[:, None] * inv_freq[None, :]
        cos_parts += [np.cos(ang), np.cos(ang)]
        sin_parts += [-np.sin(ang), np.sin(ang)]
    return (jnp.asarray(np.concatenate(cos_parts, axis=-1), F32),
            jnp.asarray(np.concatenate(sin_parts, axis=-1), F32))


def _band_bias():
    key = np.arange(ATTN_KEY_SPAN)[:, None]
    qry = (np.arange(GROUP * ATTN_Q_TILE) % ATTN_Q_TILE)[None, :]
    tabs = [np.where(np.abs(key - off - qry) <= WINDOW, 0.0, NEG_INF) for off in (0, WINDOW, 2 * WINDOW)]
    return jnp.asarray(np.stack(tabs), F32)


def _attn_kernel(*refs, windowed, L, nblk):
    if windowed:
        sink_ref, q_ref, k_ref, vt_ref, ck_ref, cvt_ref, bias_ref, o_ref, s_scr, p_scr, d_scr = refs
    else:
        sink_ref, q_ref, k_ref, vt_ref, o_ref, s_scr, p_scr, d_scr = refs
    tq = ATTN_Q_TILE
    ncol = GROUP * tq
    nkeys = s_scr.shape[1]
    nt = (((1,), (1,)), ((), ()))
    pairs = [(jb, kk) for jb in range(nblk) for kk in range(N_KV_HEADS)]

    def place(c):
        jb, kk = pairs[c]
        rows = slice(jb * tq, (jb + 1) * tq)
        kv = slice(kk * HEAD_DIM, (kk + 1) * HEAD_DIM)
        heads = [kk * GROUP + g for g in range(GROUP)]
        qs = (pl.program_id(1) * nblk + jb) * tq
        start = pl.multiple_of(jnp.clip(qs - WINDOW, 0, L - ATTN_KEY_SPAN), WINDOW) if windowed else 0
        return rows, kv, heads, qs, start

    def scores(c):
        rows, kv, heads, qs, start = place(c)
        q4 = jnp.concatenate([q_ref[rows, hd * HEAD_DIM:(hd + 1) * HEAD_DIM] for hd in heads], axis=0)
        if windowed:
            s_scr[c % 2, :ATTN_KEY_SPAN] = lax.dot_general(
                k_ref[pl.ds(start, ATTN_KEY_SPAN), kv], q4, nt,
                preferred_element_type=F32) + bias_ref[(qs - start) // WINDOW]
            s_scr[c % 2, ATTN_KEY_SPAN:] = lax.dot_general(
                ck_ref[:, kv].astype(BF16), q4, nt, preferred_element_type=F32)
        else:
            s_scr[c % 2] = lax.dot_general(k_ref[:, kv].astype(BF16), q4, nt, preferred_element_type=F32)

    def fold(acc, blk, op):
        for r in range(0, ATTN_KEY_CHUNK, 8):
            acc = op(acc, blk[r:r + 8])
        return acc

    def softmax(c):
        _, _, heads, _, _ = place(c)
        slot = c % 2
        sink = jnp.concatenate([jnp.full((1, tq), sink_ref[hd] * LOG2E, F32) for hd in heads], axis=1)
        m8 = jnp.full((8, ncol), NEG_INF, F32)
        for r in range(0, nkeys, ATTN_KEY_CHUNK):
            m8 = fold(m8, s_scr[slot, r:r + ATTN_KEY_CHUNK, :], jnp.maximum)
        m = jnp.maximum(jnp.max(m8, axis=0, keepdims=True), sink)
        l8 = jnp.zeros((8, ncol), F32)
        for r in range(0, nkeys, ATTN_KEY_CHUNK):
            p = jnp.exp2(s_scr[slot, r:r + ATTN_KEY_CHUNK, :] - m)
            l8 = fold(l8, p, jnp.add)
            p_scr[slot, r:r + ATTN_KEY_CHUNK, :] = p.astype(BF16)
        denom = jnp.sum(l8, axis=0, keepdims=True) + jnp.exp2(sink - m)
        d_scr[slot] = jnp.broadcast_to(1.0 / denom, (8, ncol))

    def values(c):
        rows, kv, heads, _, start = place(c)
        slot = c % 2
        if windowed:
            o = (_dot(vt_ref[kv, pl.ds(start, ATTN_KEY_SPAN)], p_scr[slot, :ATTN_KEY_SPAN])
                 + _dot(cvt_ref[kv, :].astype(BF16), p_scr[slot, ATTN_KEY_SPAN:]))
        else:
            o = _dot(vt_ref[kv, :], p_scr[slot])
        o = (o * d_scr[slot, 0:1, :]).T
        for g, hd in enumerate(heads):
            o_ref[rows, hd * HEAD_DIM:(hd + 1) * HEAD_DIM] = o[g * tq:(g + 1) * tq].astype(o_ref.dtype)

    n = len(pairs)
    scores(0)
    scores(1)
    softmax(0)
    for c in range(n):
        if c + 2 < n:
            scores(c + 2)
        if c + 1 < n:
            softmax(c + 1)
        values(c)


def _attention(q, k, vt, sink, B, L, ctx_k=None, ctx_vt=None):
    windowed = ctx_k is not None
    nblk = min(L // ATTN_Q_TILE, 4)
    tstep = nblk * ATTN_Q_TILE
    nq = L // tstep
    sink_spec = pl.BlockSpec(memory_space=pltpu.SMEM)
    q_spec = pl.BlockSpec((tstep, ATTN_WIDTH), lambda b, i: (b * nq + i, 0))
    in_specs = [sink_spec, q_spec,
                pl.BlockSpec((L, KV_WIDTH), lambda b, i: (b, 0)),
                pl.BlockSpec((KV_WIDTH, L), lambda b, i: (0, b))]
    args = [sink, q, k, vt]
    nkeys, ncol = L, GROUP * ATTN_Q_TILE
    if windowed:
        assert L >= ATTN_KEY_SPAN + WINDOW
        P = ctx_k.shape[0] // B
        nkeys = ATTN_KEY_SPAN + P
        bias = _band_bias()
        in_specs += [pl.BlockSpec((P, KV_WIDTH), lambda b, i: (b, 0)),
                     pl.BlockSpec((KV_WIDTH, P), lambda b, i: (b, 0)),
                     _const_spec(bias.shape)]
        args += [ctx_k, ctx_vt, bias]
    return pl.pallas_call(
        functools.partial(_attn_kernel, windowed=windowed, L=L, nblk=nblk),
        grid=(B, nq),
        in_specs=in_specs,
        out_specs=q_spec,
        out_shape=jax.ShapeDtypeStruct((B * L, ATTN_WIDTH), BF16),
        scratch_shapes=[pltpu.VMEM((2, nkeys, ncol), F32), pltpu.VMEM((2, nkeys, ncol), BF16),
                        pltpu.VMEM((2, 8, ncol), F32)],
        compiler_params=_params("arbitrary", "arbitrary"),
        name=f"attention_{L}",
    )(*args)


def _hyena_kernel(x0_ref, x1_ref, v_ref, cw0_ref, cw1_ref, cwv_ref, cb0_ref, cb1_ref, cbv_ref,
                  hr_ref, hi_ref, hn_ref, skip_ref, cos_ref, sin_ref, o_ref, *, L, nseq):
    rows, tc = o_ref.shape
    pos = lax.broadcasted_iota(jnp.int32, (rows, tc), 0) & (L - 1)
    first = pos == 0
    last = pos == L - 1

    def short_conv(u_ref, w_ref, b_ref):
        u = u_ref[...].astype(F32)
        prev = jnp.where(first, 0.0, pltpu.roll(u, 1, axis=0))
        nxt = jnp.where(last, 0.0, pltpu.roll(u, rows - 1, axis=0))
        return prev * w_ref[0:1, :] + u * w_ref[1:2, :] + nxt * w_ref[2:3, :] + b_ref[...]

    def by_seq(a):
        return a if nseq == 1 else jnp.concatenate([a[s * L:(s + 1) * L] for s in range(nseq)], axis=1)

    def by_row(a):
        return a if nseq == 1 else jnp.concatenate([a[:, s * tc:(s + 1) * tc] for s in range(nseq)], axis=0)

    def tiled(a):
        return a if nseq == 1 else jnp.concatenate([a] * nseq, axis=1)

    z = by_seq(short_conv(v_ref, cwv_ref, cbv_ref) * short_conv(x1_ref, cw1_ref, cb1_ref))
    zb = z.astype(BF16)
    sign = jnp.where((lax.broadcasted_iota(jnp.int32, z.shape, 0) & 1) == 0, 1.0, -1.0)
    z_nyq = jnp.sum(z * sign, axis=0, keepdims=True)
    zc = _dot(cos_ref[...], zb)
    zs = _dot(sin_ref[...], zb)
    hr = tiled(hr_ref[...])
    hi = tiled(hi_ref[...])
    wr = (zc * hr + zs * hi).astype(BF16)
    wi = (zc * hi - zs * hr).astype(BF16)
    y = _dot(cos_ref[...], wr) - _dot(sin_ref[...], wi)
    y = y + sign * (z_nyq * tiled(hn_ref[...])) + z * tiled(skip_ref[...])
    o_ref[...] = (short_conv(x0_ref, cw0_ref, cb0_ref) * by_row(y)).astype(o_ref.dtype)


def _hyena(hy, conv_w, conv_b, hr, hi, hn, skip, cos_t, sin_t, B, L):
    C = HYENA_WIDTH
    tc = HYENA_CH_TILE
    nj = C // tc
    assert L & (L - 1) == 0
    nseq = min(B, max(1, HYENA_ROWS_PER_STEP // L))

    def part(p, rows):
        return pl.BlockSpec((rows, tc), lambda b, j: (b if rows == nseq * L else 0, p * nj + j))

    spec_c = pl.BlockSpec((L, tc), lambda b, j: (0, j))
    vec_c = pl.BlockSpec((1, tc), lambda b, j: (0, j))
    conv_b = conv_b.reshape(1, 3 * C)
    seq = nseq * L
    return pl.pallas_call(
        functools.partial(_hyena_kernel, L=L, nseq=nseq),
        grid=(B // nseq, nj),
        in_specs=[part(0, seq), part(1, seq), part(2, seq),
                  part(0, 3), part(1, 3), part(2, 3),
                  part(0, 1), part(1, 1), part(2, 1),
                  spec_c, spec_c, vec_c, vec_c,
                  _const_spec((L, L)), _const_spec((L, L))],
        out_specs=pl.BlockSpec((seq, tc), lambda b, j: (b, j)),
        out_shape=jax.ShapeDtypeStruct((B * L, C), BF16),
        compiler_params=_params("arbitrary", "arbitrary"),
        name=f"hyena_{L}",
    )(hy, hy, hy, conv_w, conv_w, conv_w, conv_b, conv_b, conv_b,
      hr, hi, hn, skip.reshape(1, C), cos_t, sin_t)


def _merge_ffn_kernel(x_ref, attn_ref, hyo_ref, mod_ref, g_pre_ref, g_post_ref, g_fpre_ref, g_fpost_ref,
                      wg_ref, bg_ref, wpa_ref, wph_ref, wo_ref, wup_ref, wdn_ref, o_ref):
    D = D_MODEL
    m = mod_ref[0]
    sh1, sc1, g1, sh2, sc2, g2 = [m[:, j * D:(j + 1) * D] for j in range(6)]
    x = x_ref[...]
    h = _modulated_norm(x, g_pre_ref[...], sh1, sc1).astype(BF16)
    gate_a = jax.nn.sigmoid(_dot(h, wg_ref[:, :D]) + bg_ref[:, :D])
    merged = gate_a * _dot(attn_ref[...], wpa_ref[...])
    gate_h = jax.nn.sigmoid(_dot(h, wg_ref[:, D:]) + bg_ref[:, D:])
    merged = merged + gate_h * _dot(hyo_ref[...], wph_ref[...])
    mix = _dot(merged.astype(BF16), wo_ref[...])
    x = x + g1 * _rmsnorm(mix, g_post_ref[...])
    h2 = _modulated_norm(x, g_fpre_ref[...], sh2, sc2).astype(BF16)
    f = jnp.zeros(x.shape, F32)
    for c0 in range(0, D_FF, FFN_CHUNK):
        gt = _dot(h2, wup_ref[:, c0:c0 + FFN_CHUNK])
        up = _dot(h2, wup_ref[:, D_FF + c0:D_FF + c0 + FFN_CHUNK])
        f = f + _dot((_silu(gt) * up).astype(BF16), wdn_ref[c0:c0 + FFN_CHUNK, :])
    o_ref[...] = x + g2 * _rmsnorm(f, g_fpost_ref[...])


def _merge_ffn(x, attn, hyo, mod, norms, w_gate, b_gate, w_pa, w_ph, w_o, w_up, w_down, L):
    R = x.shape[0]
    tm = ROW_TILE
    steps_per_mod = (R // mod.shape[0]) // tm
    row = lambda i: (i, 0)
    consts = list(norms) + [w_gate, b_gate, w_pa, w_ph, w_o, w_up, w_down]
    return pl.pallas_call(
        _merge_ffn_kernel,
        grid=(R // tm,),
        in_specs=[pl.BlockSpec((tm, D_MODEL), row), pl.BlockSpec((tm, ATTN_WIDTH), row),
                  pl.BlockSpec((tm, HYENA_WIDTH), row),
                  pl.BlockSpec((1, 1, mod.shape[2]), lambda i: (i // steps_per_mod, 0, 0))]
                 + [_const_spec(a.shape) for a in consts],
        out_specs=pl.BlockSpec((tm, D_MODEL), row),
        out_shape=jax.ShapeDtypeStruct((R, D_MODEL), F32),
        compiler_params=_params("arbitrary"),
        name=f"merge_ffn_{L}",
    )(x, attn, hyo, mod, *consts)


def _filter_features(L):
    t = np.arange(L, dtype=np.float64)
    t_norm = t / L
    bands = np.linspace(1e-4, FILTER_BANDS - 1, FILTER_BANDS)
    ang = (2.0 * math.pi * t / L)[:, None] * bands[None, :]
    feat = np.concatenate([t_norm[:, None], np.cos(ang), np.sin(ang)], axis=-1)
    feat = np.pad(feat, ((0, 0), (0, FILTER_FEAT_PAD - FILTER_FEAT)))
    return jnp.asarray(feat.T, F32), jnp.asarray(t_norm[:, None], F32)


def _layer(x, mod, p, B, L, latent, ctx_k, ctx_v):
    feat, tnorm = _filter_features(L)
    hr, hi, hn, cos_t, sin_t = _spectrum(L, feat, tnorm, p['filt_w1'], p['filt_b1'], p['filt_freq1'],
                                         p['filt_w2'], p['filt_b2'], p['filt_freq2'], p['filt_w3'],
                                         p['filt_b3'], p['filt_decay'])
    rope_tabs = _rope_tables(L) if latent else None
    q, k, v, vt, hy = _inproj(x, mod, p['norm_mix_pre'], p['w_in'], L, rope_tabs, BF16 if latent else F32)
    attn = _attention(q, k, vt, p['attn_sink'], B, L, ctx_k, ctx_v)
    hyo = _hyena(hy, p['conv_w'], p['conv_b'], hr, hi, hn, p['hyena_skip'], cos_t, sin_t, B, L)
    norms = [p['norm_mix_pre'], p['norm_mix_post'], p['norm_ffn_pre'], p['norm_ffn_post']]
    y = _merge_ffn(x, attn, hyo, mod, norms, p['w_gate'], p['b_gate'], p['w_pa'], p['w_ph'], p['w_o'],
                   p['w_up'], p['w_down'], L)
    return y, k, v


def kernel(x_prompt, x_sample, c, cache_k, cache_v, c_ctx, norm_mix_pre, norm_mix_post, norm_ffn_pre, norm_ffn_post, w_mod, b_mod, w_in, attn_sink, conv_w, conv_b, filt_w1, filt_b1, filt_freq1, filt_w2, filt_b2, filt_freq2, filt_w3, filt_b3, filt_decay, hyena_skip, w_pa, w_ph, w_gate, b_gate, w_o, w_up, w_down):
    Bp, Lp, D = x_prompt.shape
    Bs, Ls, _ = x_sample.shape
    depth = w_in.shape[0]
    past = cache_k.shape[2]
    yp = x_prompt.reshape(Bp * Lp, D)
    ys = x_sample.reshape(Bs * Ls, D)
    cond = jnp.concatenate([c_ctx[None, :], c, jnp.zeros((8 - 1 - Bs, D), F32)], axis=0)
    new_k, new_v = [], []
    row2 = lambda a: a.reshape(1, -1)
    col2 = lambda a: a.reshape(-1, 1)
    for l in range(depth):
        w1 = jnp.pad(filt_w1[l], ((0, FILTER_FEAT_PAD - FILTER_FEAT), (0, 0)))
        p = {
            'norm_mix_pre': row2(norm_mix_pre[l]), 'norm_mix_post': row2(norm_mix_post[l]),
            'norm_ffn_pre': row2(norm_ffn_pre[l]), 'norm_ffn_post': row2(norm_ffn_post[l]),
            'w_in': w_in[l].astype(BF16), 'attn_sink': attn_sink[l],
            'conv_w': conv_w[l], 'conv_b': conv_b[l],
            'filt_w1': w1.T, 'filt_b1': col2(filt_b1[l]), 'filt_freq1': col2(filt_freq1[l]),
            'filt_w2': filt_w2[l].T, 'filt_b2': col2(filt_b2[l]), 'filt_freq2': col2(filt_freq2[l]),
            'filt_w3': filt_w3[l], 'filt_b3': row2(filt_b3[l]), 'filt_decay': row2(filt_decay[l]),
            'hyena_skip': hyena_skip[l], 'w_pa': w_pa[l].astype(BF16), 'w_ph': w_ph[l].astype(BF16),
            'w_gate': w_gate[l].astype(BF16), 'b_gate': row2(b_gate[l]), 'w_o': w_o[l].astype(BF16),
            'w_up': w_up[l].astype(BF16), 'w_down': w_down[l].astype(BF16),
        }
        mod = _modulation(cond, w_mod[l], b_mod[l])
        mod_p = mod[0:1].reshape(1, 1, 6 * D)
        mod_s = mod[1:1 + Bs].reshape(Bs, 1, 6 * D)
        yp, k_ctx, v_ctx = _layer(yp, mod_p, p, Bp, Lp, False, None, None)
        new_k.append(k_ctx.reshape(Bp, Lp, N_KV_HEADS, HEAD_DIM))
        new_v.append(v_ctx.reshape(Bp, Lp, N_KV_HEADS, HEAD_DIM))
        ck = cache_k[:, l].reshape(Bs * past, KV_WIDTH)
        cv = jnp.swapaxes(cache_v[:, l].reshape(Bs, past, KV_WIDTH), 1, 2).reshape(Bs * KV_WIDTH, past)
        ys, _, _ = _layer(ys, mod_s, p, Bs, Ls, True, ck, cv)
    return (yp.reshape(Bp, Lp, D), ys.reshape(Bs, Ls, D),
            jnp.stack(new_k, axis=1), jnp.stack(new_v, axis=1))
```

```python
import functools
import math

import numpy as np
import jax
import jax.numpy as jnp
from jax import lax
from jax.experimental import pallas as pl
from jax.experimental.pallas import tpu as pltpu

D_MODEL = 1024
GRID_W = 64
N_HEADS = 8
N_KV_HEADS = 2
HEAD_DIM = 128
GROUP = N_HEADS // N_KV_HEADS
ATTN_WIDTH = N_HEADS * HEAD_DIM
KV_WIDTH = N_KV_HEADS * HEAD_DIM
WINDOW = 128
ATTN_SCALE = HEAD_DIM ** -0.5
LOG2E = math.log2(math.e)
ROPE_THETA = 10000.0
HYENA_WIDTH = D_MODEL // 2
FILTER_BANDS = 8
FILTER_FEAT = 1 + 2 * FILTER_BANDS
FILTER_FEAT_PAD = 32
FILTER_HIDDEN = 64
D_FF = -(-8 * D_MODEL // (3 * 256)) * 256
IN_WIDTH = ATTN_WIDTH + 2 * KV_WIDTH + 3 * HYENA_WIDTH
EPS = 1e-6
NEG_INF = -1e30

V7X_VMEM_BYTES = 64 * 1024 * 1024
VMEM_LIMIT = V7X_VMEM_BYTES - 8 * 1024 * 1024
MXU_COLS = 256
LANES = 128
FFN_CHUNK = MXU_COLS
ROW_TILE = 512
ATTN_KEY_CHUNK = 64
ATTN_Q_TILE = 128
ATTN_KEY_SPAN = ATTN_Q_TILE + 2 * WINDOW
DFT_FINE_ROWS = 32
HYENA_CH_TILE = 256
HYENA_ROWS_PER_STEP = 2048

BF16 = jnp.bfloat16
F32 = jnp.float32


def _params(*semantics):
    return pltpu.CompilerParams(dimension_semantics=semantics, vmem_limit_bytes=VMEM_LIMIT)


def _const_spec(shape):
    zeros = (0,) * len(shape)
    return pl.BlockSpec(shape, lambda *_: zeros, pipeline_mode=pl.Buffered(1))


def _dot(a, b):
    return jnp.dot(a, b, preferred_element_type=F32)


def _rmsnorm(x, g):
    return x * lax.rsqrt(jnp.mean(x * x, axis=-1, keepdims=True) + EPS) * g


def _silu(x):
    return x * jax.nn.sigmoid(x)


def _mod_kernel(cond_ref, w_ref, b_ref, o_ref):
    a = _silu(cond_ref[...]).astype(BF16)
    o_ref[...] = _dot(a, w_ref[...].astype(BF16)) + b_ref[...]


def _modulation(cond, w_mod, b_mod):
    rows = cond.shape[0]
    n = w_mod.shape[1]
    tn = D_MODEL
    return pl.pallas_call(
        _mod_kernel,
        grid=(n // tn,),
        in_specs=[pl.BlockSpec((rows, D_MODEL), lambda j: (0, 0)),
                  pl.BlockSpec((D_MODEL, tn), lambda j: (0, j)),
                  pl.BlockSpec((1, tn), lambda j: (0, j))],
        out_specs=pl.BlockSpec((rows, tn), lambda j: (0, j)),
        out_shape=jax.ShapeDtypeStruct((rows, n), F32),
        compiler_params=_params("arbitrary"),
        name="modulation",
    )(cond, w_mod, b_mod.reshape(1, n))


def _dft_seed_tables(n):
    bs = DFT_FINE_ROWS
    t = np.arange(n, dtype=np.int64)
    theta = math.pi / n
    ang_a = ((np.arange(n // bs, dtype=np.int64)[:, None] * bs * t[None, :]) % (2 * n)) * theta
    ang_b = ((np.arange(bs, dtype=np.int64)[:, None] * t[None, :]) % (2 * n)) * theta
    return [jnp.asarray(f(a), F32) for a in (ang_a, ang_b) for f in (np.cos, np.sin)]


def _twiddles(L):
    ang = (math.pi / L) * np.arange(L // 2, dtype=np.float64)[:, None] * np.ones((1, HYENA_WIDTH))
    return jnp.asarray(np.cos(ang), F32), jnp.asarray(np.sin(ang), F32)


def _spectrum_kernel(feat_ref, w1_ref, b1_ref, f1_ref, w2_ref, b2_ref, f2_ref, w3_ref, b3_ref,
                     decay_ref, tn_ref, ca_ref, sa_ref, cb_ref, sb_ref, twc_ref, tws_ref,
                     hlr_ref, hli_ref, hhr_ref, hhi_ref, hm_ref, cos_ref, sin_ref, opc_s, ops_s, *, L, tk):
    i = pl.program_id(0)
    hi_prec = lax.Precision.HIGHEST
    C = HYENA_WIDTH
    half = L // 2

    bs = DFT_FINE_ROWS
    cb, sb = cb_ref[...], sb_ref[...]
    for j in range(tk // bs):
        ca = ca_ref[pl.ds(i * (tk // bs) + j, 1), :]
        sa = sa_ref[pl.ds(i * (tk // bs) + j, 1), :]
        cos_ref[j * bs:(j + 1) * bs, :] = (ca * cb - sa * sb).astype(BF16)
        sin_ref[j * bs:(j + 1) * bs, :] = (sa * cb + ca * sb).astype(BF16)

    @pl.when(i == 0)
    def _():
        h = jnp.dot(w1_ref[...], feat_ref[...], precision=hi_prec, preferred_element_type=F32)
        h = jnp.sin(f1_ref[...] * (h + b1_ref[...]))
        h = jnp.dot(w2_ref[...], h, precision=hi_prec, preferred_element_type=F32)
        h = jnp.sin(f2_ref[...] * (h + b2_ref[...]))
        h = _dot(h.T.astype(BF16), w3_ref[...].astype(BF16)) + b3_ref[...]
        h = h * jnp.exp(-tn_ref[...] * jnp.abs(decay_ref[...]))
        fwd = h[:, :C]
        bwd = h[:, C:]
        row = lax.broadcasted_iota(jnp.int32, (L, C), 0)
        bwd = jnp.where(row == 0, 0.0, bwd)
        a = fwd + bwd
        b = fwd - bwd
        a_e, a_o, b_e, b_o = a[:half], a[half:], b[:half], b[half:]
        opc_s[...] = jnp.concatenate([a_e, a_o, b_o], axis=1).astype(BF16)
        ops_s[...] = jnp.concatenate([a_o, b_e, b_o], axis=1).astype(BF16)
        sign = jnp.where((lax.broadcasted_iota(jnp.int32, (half, C), 0) & 1) == 0, 1.0, -1.0)
        mid_r = jnp.sum(a_e * sign, axis=0, keepdims=True)
        mid_i = -jnp.sum(b_o * sign, axis=0, keepdims=True)
        hm_ref[...] = jnp.concatenate([mid_r, mid_i, jnp.zeros((6, C), F32)], axis=0) * (1.0 / L)

    pc = _dot(cos_ref[...], opc_s[...])
    ps = _dot(sin_ref[...], ops_s[...])
    ae_c, ao_c, bo_c = pc[:, :C], pc[:, C:2 * C], pc[:, 2 * C:]
    ao_s, be_s, bo_s = ps[:, :C], ps[:, C:2 * C], ps[:, 2 * C:]
    tw_c, tw_s = twc_ref[...], tws_ref[...]
    t_r = tw_c * ao_c - tw_s * ao_s
    t_i = tw_c * bo_s + tw_s * bo_c
    krow = lax.broadcasted_iota(jnp.int32, (tk, C), 0) + i * tk
    wk = jnp.where(krow == 0, 1.0, 2.0) * (1.0 / (2 * L))
    hlr_ref[...] = (ae_c + t_r) * wk
    hhr_ref[...] = (ae_c - t_r) * wk
    hli_ref[...] = (be_s + t_i) * (-wk)
    hhi_ref[...] = (be_s - t_i) * wk


def _spectrum(L, feat, tnorm, twiddles, w1, b1, f1, w2, b2, f2, w3, b3, decay):
    half = L // 2
    tk = min(half, 512)
    C = HYENA_WIDTH
    small = [feat, w1, b1, f1, w2, b2, f2, w3, b3, decay, tnorm] + _dft_seed_tables(half)
    blk = lambda w: pl.BlockSpec((tk, w), lambda i: (i, 0))
    return pl.pallas_call(
        functools.partial(_spectrum_kernel, L=L, tk=tk),
        grid=(half // tk,),
        in_specs=[_const_spec(a.shape) for a in small] + [blk(C), blk(C)],
        out_specs=[blk(C), blk(C), blk(C), blk(C), pl.BlockSpec((8, C), lambda i: (0, 0)), blk(half), blk(half)],
        out_shape=[jax.ShapeDtypeStruct((half, C), F32)] * 4
                  + [jax.ShapeDtypeStruct((8, C), F32),
                     jax.ShapeDtypeStruct((half, half), BF16), jax.ShapeDtypeStruct((half, half), BF16)],
        scratch_shapes=[pltpu.VMEM((half, 3 * C), BF16), pltpu.VMEM((half, 3 * C), BF16)],
        compiler_params=_params("arbitrary"),
        name=f"spectrum_{L}",
    )(*small, *twiddles)


def _modulated_norm(x, g, shift, scale):
    return _rmsnorm(x, g) * (1.0 + scale) + shift


def _rope(x, cos, sin_signed):
    lane = lax.broadcasted_iota(jnp.int32, x.shape, 1)
    swapped = jnp.where((lane & 63) < 32,
                        pltpu.roll(x, HEAD_DIM - 32, axis=1),
                        pltpu.roll(x, 32, axis=1))
    return x * cos + swapped * sin_signed


def _inproj_kernel(*refs, rope):
    if rope:
        x_ref, mod_ref, g_ref, w_ref, cos_ref, sin_ref, q_ref, k_ref, v_ref, vt_ref, hy_ref = refs
    else:
        x_ref, mod_ref, g_ref, w_ref, q_ref, k_ref, v_ref, vt_ref, hy_ref = refs
    m = mod_ref[0]
    h = _modulated_norm(x_ref[...], g_ref[...], m[:, 0:D_MODEL], m[:, D_MODEL:2 * D_MODEL]).astype(BF16)

    def heads(y, out_ref, n, scale):
        if not rope:
            out_ref[...] = (y if scale == 1.0 else y * scale).astype(out_ref.dtype)
            return
        cos, sin = cos_ref[...], sin_ref[...]
        if scale != 1.0:
            cos, sin = cos * scale, sin * scale
        for hd in range(n):
            cols = slice(hd * HEAD_DIM, (hd + 1) * HEAD_DIM)
            out_ref[:, cols] = _rope(y[:, cols], cos, sin).astype(out_ref.dtype)

    heads(_dot(h, w_ref[:, :ATTN_WIDTH]), q_ref, N_HEADS, ATTN_SCALE * LOG2E)
    kv = _dot(h, w_ref[:, ATTN_WIDTH:ATTN_WIDTH + 2 * KV_WIDTH])
    heads(kv[:, :KV_WIDTH], k_ref, N_KV_HEADS, 1.0)
    v = kv[:, KV_WIDTH:]
    v_ref[...] = v.astype(v_ref.dtype)
    vt_ref[...] = v.T.astype(vt_ref.dtype)
    hy_ref[...] = _dot(h, w_ref[:, ATTN_WIDTH + 2 * KV_WIDTH:]).astype(hy_ref.dtype)


def _inproj(x, mod, g_pre, w_in, L, rope_tabs, kv_dtype):
    R = x.shape[0]
    tm = ROW_TILE
    steps_per_mod = (R // mod.shape[0]) // tm
    row = lambda i: (i, 0)
    in_specs = [pl.BlockSpec((tm, D_MODEL), row),
                pl.BlockSpec((1, 1, mod.shape[2]), lambda i: (i // steps_per_mod, 0, 0)),
                _const_spec(g_pre.shape), _const_spec(w_in.shape)]
    args = [x, mod, g_pre, w_in]
    if rope_tabs is not None:
        steps_per_seq = L // tm
        tab = pl.BlockSpec((tm, HEAD_DIM), lambda i: (i % steps_per_seq, 0))
        in_specs += [tab, tab]
        args += list(rope_tabs)
    return pl.pallas_call(
        functools.partial(_inproj_kernel, rope=rope_tabs is not None),
        grid=(R // tm,),
        in_specs=in_specs,
        out_specs=[pl.BlockSpec((tm, ATTN_WIDTH), row), pl.BlockSpec((tm, KV_WIDTH), row),
                   pl.BlockSpec((tm, KV_WIDTH), row), pl.BlockSpec((KV_WIDTH, tm), lambda i: (0, i)),
                   pl.BlockSpec((tm, 3 * HYENA_WIDTH), row)],
        out_shape=[jax.ShapeDtypeStruct((R, ATTN_WIDTH), BF16), jax.ShapeDtypeStruct((R, KV_WIDTH), kv_dtype),
                   jax.ShapeDtypeStruct((R, KV_WIDTH), kv_dtype), jax.ShapeDtypeStruct((KV_WIDTH, R), BF16),
                   jax.ShapeDtypeStruct((R, 3 * HYENA_WIDTH), BF16)],
        compiler_params=_params("arbitrary"),
        name=f"inproj_{L}",
    )(*args)


def _rope_tables(L):
    rows = L // GRID_W
    row_ids = np.repeat(np.arange(rows), GRID_W)
    col_ids = np.tile(np.arange(GRID_W), rows)
    half = HEAD_DIM // 2
    inv_freq = ROPE_THETA ** (-np.arange(0, half, 2, dtype=np.float64) / half)
    cos_parts, sin_parts = [], []
    for pos in (row_ids, col_ids):
        ang = pos.astype(np.float64)[:, None] * inv_freq[None, :]
        cos_parts += [np.cos(ang), np.cos(ang)]
        sin_parts += [-np.sin(ang), np.sin(ang)]
    return (jnp.asarray(np.concatenate(cos_parts, axis=-1), F32),
            jnp.asarray(np.concatenate(sin_parts, axis=-1), F32))


def _band_bias():
    key = np.arange(ATTN_KEY_SPAN)[:, None]
    qry = (np.arange(GROUP * ATTN_Q_TILE) % ATTN_Q_TILE)[None, :]
    tabs = [np.where(np.abs(key - off - qry) <= WINDOW, 0.0, NEG_INF) for off in (0, WINDOW, 2 * WINDOW)]
    return jnp.asarray(np.stack(tabs), F32)


def _attn_kernel(*refs, windowed, L, nblk):
    if windowed:
        sink_ref, q_ref, k_ref, vt_ref, ck_ref, cvt_ref, bias_ref, o_ref, s_scr, p_scr, d_scr = refs
    else:
        sink_ref, q_ref, k_ref, vt_ref, o_ref, s_scr, p_scr, d_scr = refs
    tq = ATTN_Q_TILE
    ncol = GROUP * tq
    nkeys = s_scr.shape[1]
    nt = (((1,), (1,)), ((), ()))
    pairs = [(jb, kk) for jb in range(nblk) for kk in range(N_KV_HEADS)]

    def place(c):
        jb, kk = pairs[c]
        rows = slice(jb * tq, (jb + 1) * tq)
        kv = slice(kk * HEAD_DIM, (kk + 1) * HEAD_DIM)
        heads = [kk * GROUP + g for g in range(GROUP)]
        qs = (pl.program_id(1) * nblk + jb) * tq
        start = pl.multiple_of(jnp.clip(qs - WINDOW, 0, L - ATTN_KEY_SPAN), WINDOW) if windowed else 0
        return rows, kv, heads, qs, start

    def scores(c):
        rows, kv, heads, qs, start = place(c)
        q4 = jnp.concatenate([q_ref[rows, hd * HEAD_DIM:(hd + 1) * HEAD_DIM] for hd in heads], axis=0)
        if windowed:
            s_scr[c % 2, :ATTN_KEY_SPAN] = lax.dot_general(
                k_ref[pl.ds(start, ATTN_KEY_SPAN), kv], q4, nt,
                preferred_element_type=F32) + bias_ref[(qs - start) // WINDOW]
            s_scr[c % 2, ATTN_KEY_SPAN:] = lax.dot_general(
                ck_ref[:, kv].astype(BF16), q4, nt, preferred_element_type=F32)
        else:
            s_scr[c % 2] = lax.dot_general(k_ref[:, kv].astype(BF16), q4, nt, preferred_element_type=F32)

    def fold(acc, blk, op):
        for r in range(0, ATTN_KEY_CHUNK, 8):
            acc = op(acc, blk[r:r + 8])
        return acc

    def softmax(c):
        _, _, heads, _, _ = place(c)
        slot = c % 2
        sink = jnp.concatenate([jnp.full((1, tq), sink_ref[hd] * LOG2E, F32) for hd in heads], axis=1)
        m8 = jnp.full((8, ncol), NEG_INF, F32)
        for r in range(0, nkeys, ATTN_KEY_CHUNK):
            m8 = fold(m8, s_scr[slot, r:r + ATTN_KEY_CHUNK, :], jnp.maximum)
        m = jnp.maximum(jnp.max(m8, axis=0, keepdims=True), sink)
        l8 = jnp.zeros((8, ncol), F32)
        for r in range(0, nkeys, ATTN_KEY_CHUNK):
            p = jnp.exp2(s_scr[slot, r:r + ATTN_KEY_CHUNK, :] - m)
            l8 = fold(l8, p, jnp.add)
            p_scr[slot, r:r + ATTN_KEY_CHUNK, :] = p.astype(BF16)
        denom = jnp.sum(l8, axis=0, keepdims=True) + jnp.exp2(sink - m)
        d_scr[slot] = jnp.broadcast_to(1.0 / denom, (8, ncol))

    def values(c):
        rows, kv, heads, _, start = place(c)
        slot = c % 2
        if windowed:
            o = (_dot(vt_ref[kv, pl.ds(start, ATTN_KEY_SPAN)], p_scr[slot, :ATTN_KEY_SPAN])
                 + _dot(cvt_ref[kv, :].astype(BF16), p_scr[slot, ATTN_KEY_SPAN:]))
        else:
            o = _dot(vt_ref[kv, :], p_scr[slot])
        o = (o * d_scr[slot, 0:1, :]).T
        for g, hd in enumerate(heads):
            o_ref[rows, hd * HEAD_DIM:(hd + 1) * HEAD_DIM] = o[g * tq:(g + 1) * tq].astype(o_ref.dtype)

    n = len(pairs)
    scores(0)
    scores(1)
    softmax(0)
    for c in range(n):
        if c + 2 < n:
            scores(c + 2)
        if c + 1 < n:
            softmax(c + 1)
        values(c)


def _attention(q, k, vt, sink, B, L, ctx_k=None, ctx_vt=None):
    windowed = ctx_k is not None
    nblk = min(L // ATTN_Q_TILE, 4)
    tstep = nblk * ATTN_Q_TILE
    nq = L // tstep
    sink_spec = pl.BlockSpec(memory_space=pltpu.SMEM)
    q_spec = pl.BlockSpec((tstep, ATTN_WIDTH), lambda b, i: (b * nq + i, 0))
    in_specs = [sink_spec, q_spec,
                pl.BlockSpec((L, KV_WIDTH), lambda b, i: (b, 0)),
                pl.BlockSpec((KV_WIDTH, L), lambda b, i: (0, b))]
    args = [sink, q, k, vt]
    nkeys, ncol = L, GROUP * ATTN_Q_TILE
    if windowed:
        assert L >= ATTN_KEY_SPAN + WINDOW
        P = ctx_k.shape[0] // B
        nkeys = ATTN_KEY_SPAN + P
        bias = _band_bias()
        in_specs += [pl.BlockSpec((P, KV_WIDTH), lambda b, i: (b, 0)),
                     pl.BlockSpec((KV_WIDTH, P), lambda b, i: (b, 0)),
                     _const_spec(bias.shape)]
        args += [ctx_k, ctx_vt, bias]
    return pl.pallas_call(
        functools.partial(_attn_kernel, windowed=windowed, L=L, nblk=nblk),
        grid=(B, nq),
        in_specs=in_specs,
        out_specs=q_spec,
        out_shape=jax.ShapeDtypeStruct((B * L, ATTN_WIDTH), BF16),
        scratch_shapes=[pltpu.VMEM((2, nkeys, ncol), F32), pltpu.VMEM((2, nkeys, ncol), BF16),
                        pltpu.VMEM((2, 8, ncol), F32)],
        compiler_params=_params("arbitrary", "arbitrary"),
        name=f"attention_{L}",
    )(*args)


def _hyena_kernel(x0_ref, x1_ref, v_ref, cw0_ref, cw1_ref, cwv_ref, cb0_ref, cb1_ref, cbv_ref,
                  hlr_ref, hli_ref, hhr_ref, hhi_ref, hm_ref, twc_ref, tws_ref, skip_ref,
                  cos_ref, sin_ref, o_ref, z_scr, y_scr, *, L, nseq):
    rows, tc = o_ref.shape
    half = L // 2
    pos = lax.broadcasted_iota(jnp.int32, (rows, tc), 0) & (L - 1)
    first = pos == 0
    last = pos == L - 1

    def short_conv(u_ref, w_ref, b_ref):
        u = u_ref[...].astype(F32)
        prev = jnp.where(first, 0.0, pltpu.roll(u, 1, axis=0))
        nxt = jnp.where(last, 0.0, pltpu.roll(u, rows - 1, axis=0))
        return prev * w_ref[0:1, :] + u * w_ref[1:2, :] + nxt * w_ref[2:3, :] + b_ref[...]

    def cmul(ar, ai, br, bi):
        return ar * br - ai * bi, ar * bi + ai * br

    lanes = range(tc // LANES)

    def deinterleave(scr, start):
        return jnp.concatenate([scr[j, pl.ds(start, half, stride=2), :] for j in lanes], axis=1)

    z = short_conv(v_ref, cwv_ref, cbv_ref) * short_conv(x1_ref, cw1_ref, cb1_ref)
    for j in lanes:
        z_scr[j] = z[:, j * LANES:(j + 1) * LANES]
    parts = [deinterleave(z_scr, s * L + par) for s in range(nseq) for par in (0, 1)]
    zcat = jnp.concatenate(parts, axis=1).astype(BF16)
    fc = _dot(cos_ref[...], zcat)
    fs = _dot(sin_ref[...], zcat)
    tw_c, tw_s = twc_ref[...], tws_ref[...]
    hlr, hli, hhr, hhi = hlr_ref[...], hli_ref[...], hhr_ref[...], hhi_ref[...]
    sign = jnp.where((lax.broadcasted_iota(jnp.int32, (half, tc), 0) & 1) == 0, 1.0, -1.0)
    g_r, g_i, mids = [], [], []
    for s in range(nseq):
        ev = slice(s * 2 * tc, s * 2 * tc + tc)
        od = slice(s * 2 * tc + tc, (s + 1) * 2 * tc)
        t_r, t_i = cmul(tw_c, -tw_s, fc[:, od], -fs[:, od])
        zl_r, zl_i = fc[:, ev] + t_r, t_i - fs[:, ev]
        zh_r, zh_i = fc[:, ev] - t_r, fs[:, ev] + t_i
        wl_r, wl_i = cmul(zl_r, zl_i, hlr, hli)
        wh_r, wh_i = cmul(zh_r, zh_i, hhr, hhi)
        g1_r, g1_i = cmul(wl_r - wh_r, wl_i + wh_i, tw_c, tw_s)
        g_r += [wl_r + wh_r, g1_r]
        g_i += [wl_i - wh_i, g1_i]
        e_m = jnp.sum(parts[2 * s] * sign, axis=0, keepdims=True)
        o_m = jnp.sum(parts[2 * s + 1] * sign, axis=0, keepdims=True)
        mids.append(cmul(e_m, -o_m, hm_ref[0:1, :], hm_ref[1:2, :]))
    y2 = (_dot(cos_ref[...], jnp.concatenate(g_r, axis=1).astype(BF16))
          - _dot(sin_ref[...], jnp.concatenate(g_i, axis=1).astype(BF16)))
    for s in range(nseq):
        wm_r, wm_i = mids[s]
        y_e = y2[:, s * 2 * tc:s * 2 * tc + tc] + sign * wm_r
        y_o = y2[:, s * 2 * tc + tc:(s + 1) * 2 * tc] - sign * wm_i
        for j in lanes:
            y_scr[j, pl.ds(s * L, half, stride=2), :] = y_e[:, j * LANES:(j + 1) * LANES]
            y_scr[j, pl.ds(s * L + 1, half, stride=2), :] = y_o[:, j * LANES:(j + 1) * LANES]
    y = jnp.concatenate([y_scr[j] for j in lanes], axis=1) + z * skip_ref[...]
    o_ref[...] = (short_conv(x0_ref, cw0_ref, cb0_ref) * y).astype(o_ref.dtype)


def _hyena(hy, conv_w, conv_b, spectrum, twiddles, skip, B, L):
    hlr, hli, hhr, hhi, hm, cos_t, sin_t = spectrum
    C = HYENA_WIDTH
    tc = HYENA_CH_TILE
    nj = C // tc
    half = L // 2
    assert L & (L - 1) == 0
    nseq = min(B, max(1, HYENA_ROWS_PER_STEP // L))
    seq = nseq * L

    def part(p, rows):
        return pl.BlockSpec((rows, tc), lambda b, j: (b if rows == seq else 0, p * nj + j))

    spec_c = pl.BlockSpec((half, tc), lambda b, j: (0, j))
    conv_b = conv_b.reshape(1, 3 * C)
    return pl.pallas_call(
        functools.partial(_hyena_kernel, L=L, nseq=nseq),
        grid=(B // nseq, nj),
        in_specs=[part(0, seq), part(1, seq), part(2, seq),
                  part(0, 3), part(1, 3), part(2, 3),
                  part(0, 1), part(1, 1), part(2, 1),
                  spec_c, spec_c, spec_c, spec_c, pl.BlockSpec((8, tc), lambda b, j: (0, j)),
                  spec_c, spec_c, pl.BlockSpec((1, tc), lambda b, j: (0, j)),
                  _const_spec((half, half)), _const_spec((half, half))],
        out_specs=pl.BlockSpec((seq, tc), lambda b, j: (b, j)),
        out_shape=jax.ShapeDtypeStruct((B * L, C), BF16),
        scratch_shapes=[pltpu.VMEM((tc // LANES, seq, LANES), F32)] * 2,
        compiler_params=_params("arbitrary", "arbitrary"),
        name=f"hyena_{L}",
    )(hy, hy, hy, conv_w, conv_w, conv_w, conv_b, conv_b, conv_b,
      hlr, hli, hhr, hhi, hm, *twiddles, skip.reshape(1, C), cos_t, sin_t)


def _merge_ffn_kernel(x_ref, attn_ref, hyo_ref, mod_ref, g_pre_ref, g_post_ref, g_fpre_ref, g_fpost_ref,
                      wg_ref, bg_ref, wpa_ref, wph_ref, wo_ref, wup_ref, wdn_ref, o_ref):
    D = D_MODEL
    m = mod_ref[0]
    sh1, sc1, g1, sh2, sc2, g2 = [m[:, j * D:(j + 1) * D] for j in range(6)]
    x = x_ref[...]
    h = _modulated_norm(x, g_pre_ref[...], sh1, sc1).astype(BF16)
    gate_a = jax.nn.sigmoid(_dot(h, wg_ref[:, :D]) + bg_ref[:, :D])
    merged = gate_a * _dot(attn_ref[...], wpa_ref[...])
    gate_h = jax.nn.sigmoid(_dot(h, wg_ref[:, D:]) + bg_ref[:, D:])
    merged = merged + gate_h * _dot(hyo_ref[...], wph_ref[...])
    mix = _dot(merged.astype(BF16), wo_ref[...])
    x = x + g1 * _rmsnorm(mix, g_post_ref[...])
    h2 = _modulated_norm(x, g_fpre_ref[...], sh2, sc2).astype(BF16)
    f = jnp.zeros(x.shape, F32)
    for c0 in range(0, D_FF, FFN_CHUNK):
        gt = _dot(h2, wup_ref[:, c0:c0 + FFN_CHUNK])
        up = _dot(h2, wup_ref[:, D_FF + c0:D_FF + c0 + FFN_CHUNK])
        f = f + _dot((_silu(gt) * up).astype(BF16), wdn_ref[c0:c0 + FFN_CHUNK, :])
    o_ref[...] = x + g2 * _rmsnorm(f, g_fpost_ref[...])


def _merge_ffn(x, attn, hyo, mod, norms, w_gate, b_gate, w_pa, w_ph, w_o, w_up, w_down, L):
    R = x.shape[0]
    tm = ROW_TILE
    steps_per_mod = (R // mod.shape[0]) // tm
    row = lambda i: (i, 0)
    consts = list(norms) + [w_gate, b_gate, w_pa, w_ph, w_o, w_up, w_down]
    return pl.pallas_call(
        _merge_ffn_kernel,
        grid=(R // tm,),
        in_specs=[pl.BlockSpec((tm, D_MODEL), row), pl.BlockSpec((tm, ATTN_WIDTH), row),
                  pl.BlockSpec((tm, HYENA_WIDTH), row),
                  pl.BlockSpec((1, 1, mod.shape[2]), lambda i: (i // steps_per_mod, 0, 0))]
                 + [_const_spec(a.shape) for a in consts],
        out_specs=pl.BlockSpec((tm, D_MODEL), row),
        out_shape=jax.ShapeDtypeStruct((R, D_MODEL), F32),
        compiler_params=_params("arbitrary"),
        name=f"merge_ffn_{L}",
    )(x, attn, hyo, mod, *consts)


def _filter_features(L):
    t = np.concatenate([np.arange(0, L, 2), np.arange(1, L, 2)]).astype(np.float64)
    t_norm = t / L
    bands = np.linspace(1e-4, FILTER_BANDS - 1, FILTER_BANDS)
    ang = (2.0 * math.pi * t / L)[:, None] * bands[None, :]
    feat = np.concatenate([t_norm[:, None], np.cos(ang), np.sin(ang)], axis=-1)
    feat = np.pad(feat, ((0, 0), (0, FILTER_FEAT_PAD - FILTER_FEAT)))
    return jnp.asarray(feat.T, F32), jnp.asarray(t_norm[:, None], F32)


def _layer(x, mod, p, B, L, latent, ctx_k, ctx_v):
    feat, tnorm = _filter_features(L)
    twiddles = _twiddles(L)
    spectrum = _spectrum(L, feat, tnorm, twiddles, p['filt_w1'], p['filt_b1'], p['filt_freq1'],
                         p['filt_w2'], p['filt_b2'], p['filt_freq2'], p['filt_w3'], p['filt_b3'],
                         p['filt_decay'])
    rope_tabs = _rope_tables(L) if latent else None
    q, k, v, vt, hy = _inproj(x, mod, p['norm_mix_pre'], p['w_in'], L, rope_tabs, BF16 if latent else F32)
    attn = _attention(q, k, vt, p['attn_sink'], B, L, ctx_k, ctx_v)
    hyo = _hyena(hy, p['conv_w'], p['conv_b'], spectrum, twiddles, p['hyena_skip'], B, L)
    norms = [p['norm_mix_pre'], p['norm_mix_post'], p['norm_ffn_pre'], p['norm_ffn_post']]
    y = _merge_ffn(x, attn, hyo, mod, norms, p['w_gate'], p['b_gate'], p['w_pa'], p['w_ph'], p['w_o'],
                   p['w_up'], p['w_down'], L)
    return y, k, v


def kernel(x_prompt, x_sample, c, cache_k, cache_v, c_ctx, norm_mix_pre, norm_mix_post, norm_ffn_pre, norm_ffn_post, w_mod, b_mod, w_in, attn_sink, conv_w, conv_b, filt_w1, filt_b1, filt_freq1, filt_w2, filt_b2, filt_freq2, filt_w3, filt_b3, filt_decay, hyena_skip, w_pa, w_ph, w_gate, b_gate, w_o, w_up, w_down):
    Bp, Lp, D = x_prompt.shape
    Bs, Ls, _ = x_sample.shape
    depth = w_in.shape[0]
    past = cache_k.shape[2]
    yp = x_prompt.reshape(Bp * Lp, D)
    ys = x_sample.reshape(Bs * Ls, D)
    cond = jnp.concatenate([c_ctx[None, :], c, jnp.zeros((8 - 1 - Bs, D), F32)], axis=0)
    new_k, new_v = [], []
    row2 = lambda a: a.reshape(1, -1)
    col2 = lambda a: a.reshape(-1, 1)
    for l in range(depth):
        w1 = jnp.pad(filt_w1[l], ((0, FILTER_FEAT_PAD - FILTER_FEAT), (0, 0)))
        p = {
            'norm_mix_pre': row2(norm_mix_pre[l]), 'norm_mix_post': row2(norm_mix_post[l]),
            'norm_ffn_pre': row2(norm_ffn_pre[l]), 'norm_ffn_post': row2(norm_ffn_post[l]),
            'w_in': w_in[l].astype(BF16), 'attn_sink': attn_sink[l],
            'conv_w': conv_w[l], 'conv_b': conv_b[l],
            'filt_w1': w1.T, 'filt_b1': col2(filt_b1[l]), 'filt_freq1': col2(filt_freq1[l]),
            'filt_w2': filt_w2[l].T, 'filt_b2': col2(filt_b2[l]), 'filt_freq2': col2(filt_freq2[l]),
            'filt_w3': filt_w3[l], 'filt_b3': row2(filt_b3[l]), 'filt_decay': row2(filt_decay[l]),
            'hyena_skip': hyena_skip[l], 'w_pa': w_pa[l].astype(BF16), 'w_ph': w_ph[l].astype(BF16),
            'w_gate': w_gate[l].astype(BF16), 'b_gate': row2(b_gate[l]), 'w_o': w_o[l].astype(BF16),
            'w_up': w_up[l].astype(BF16), 'w_down': w_down[l].astype(BF16),
        }
        mod = _modulation(cond, w_mod[l], b_mod[l])
        mod_p = mod[0:1].reshape(1, 1, 6 * D)
        mod_s = mod[1:1 + Bs].reshape(Bs, 1, 6 * D)
        yp, k_ctx, v_ctx = _layer(yp, mod_p, p, Bp, Lp, False, None, None)
        new_k.append(k_ctx.reshape(Bp, Lp, N_KV_HEADS, HEAD_DIM))
        new_v.append(v_ctx.reshape(Bp, Lp, N_KV_HEADS, HEAD_DIM))
        ck = cache_k[:, l].reshape(Bs * past, KV_WIDTH)
        cv = jnp.swapaxes(cache_v[:, l].reshape(Bs, past, KV_WIDTH), 1, 2).reshape(Bs * KV_WIDTH, past)
        ys, _, _ = _layer(ys, mod_s, p, Bs, Ls, True, ck, cv)
    return (yp.reshape(Bp, Lp, D), ys.reshape(Bs, Ls, D),
            jnp.stack(new_k, axis=1), jnp.stack(new_v, axis=1))
```

```python
import functools
import math

import numpy as np
import jax
import jax.numpy as jnp
from jax import lax
from jax.experimental import pallas as pl
from jax.experimental.pallas import tpu as pltpu

D_MODEL = 1024
GRID_W = 64
N_HEADS = 8
N_KV_HEADS = 2
HEAD_DIM = 128
GROUP = N_HEADS // N_KV_HEADS
ATTN_WIDTH = N_HEADS * HEAD_DIM
KV_WIDTH = N_KV_HEADS * HEAD_DIM
WINDOW = 128
ATTN_SCALE = HEAD_DIM ** -0.5
LOG2E = math.log2(math.e)
ROPE_THETA = 10000.0
HYENA_WIDTH = D_MODEL // 2
FILTER_BANDS = 8
FILTER_FEAT = 1 + 2 * FILTER_BANDS
FILTER_FEAT_PAD = 32
FILTER_HIDDEN = 64
D_FF = -(-8 * D_MODEL // (3 * 256)) * 256
IN_WIDTH = ATTN_WIDTH + 2 * KV_WIDTH + 3 * HYENA_WIDTH
EPS = 1e-6
NEG_INF = -1e30

V7X_VMEM_BYTES = 64 * 1024 * 1024
VMEM_LIMIT = V7X_VMEM_BYTES - 8 * 1024 * 1024
MXU_COLS = 256
LANES = 128
FFN_CHUNK = MXU_COLS
ROW_TILE = 512
ATTN_KEY_CHUNK = 64
ATTN_Q_TILE = 128
ATTN_KEY_SPAN = ATTN_Q_TILE + 2 * WINDOW
DFT_FINE_ROWS = 32
HYENA_CH_TILE = 256
HYENA_ROWS_PER_STEP = 2048

BF16 = jnp.bfloat16
F32 = jnp.float32


def _params(*semantics):
    return pltpu.CompilerParams(dimension_semantics=semantics, vmem_limit_bytes=VMEM_LIMIT)


def _const_spec(shape):
    zeros = (0,) * len(shape)
    return pl.BlockSpec(shape, lambda *_: zeros, pipeline_mode=pl.Buffered(1))


def _dot(a, b):
    return jnp.dot(a, b, preferred_element_type=F32)


def _with_casts(body, n_in, n_out, ncast):
    if ncast == 0:
        return body

    def kernel(*refs):
        ins, rest = refs[:n_in], refs[n_in:]
        cast_in, rest = rest[:ncast], rest[ncast:]
        outs, rest = rest[:n_out], rest[n_out:]
        cast_out, scratch = rest[:ncast], rest[ncast:]
        for src, dst in zip(cast_in, cast_out):
            dst[...] = src[...].astype(dst.dtype)
        body(*ins, *outs, *scratch)

    return kernel


def _cast_plan(casts, nsteps, linear_step):
    specs, shapes = [], []
    for a in casts:
        rows, cols = a.shape
        slab = rows // nsteps
        assert slab * nsteps == rows and slab % 16 == 0, (a.shape, nsteps)
        specs.append(pl.BlockSpec((slab, cols), lambda *g: (linear_step(*g), 0)))
        shapes.append(jax.ShapeDtypeStruct(a.shape, BF16))
    return specs, shapes


def _rmsnorm(x, g):
    return x * lax.rsqrt(jnp.mean(x * x, axis=-1, keepdims=True) + EPS) * g


def _silu(x):
    return x * jax.nn.sigmoid(x)


def _mod_kernel(cond_ref, w_ref, b_ref, o_ref):
    a = _silu(cond_ref[...]).astype(BF16)
    o_ref[...] = _dot(a, w_ref[...].astype(BF16)) + b_ref[...]


def _modulation(cond, w_mod, b_mod):
    rows = cond.shape[0]
    n = w_mod.shape[1]
    tn = D_MODEL
    return pl.pallas_call(
        _mod_kernel,
        grid=(n // tn,),
        in_specs=[pl.BlockSpec((rows, D_MODEL), lambda j: (0, 0)),
                  pl.BlockSpec((D_MODEL, tn), lambda j: (0, j)),
                  pl.BlockSpec((1, tn), lambda j: (0, j))],
        out_specs=pl.BlockSpec((rows, tn), lambda j: (0, j)),
        out_shape=jax.ShapeDtypeStruct((rows, n), F32),
        compiler_params=_params("arbitrary"),
        name="modulation",
    )(cond, w_mod, b_mod.reshape(1, n))


def _dft_seed_tables(n):
    bs = DFT_FINE_ROWS
    t = np.arange(n, dtype=np.int64)
    theta = math.pi / n
    ang_a = ((np.arange(n // bs, dtype=np.int64)[:, None] * bs * t[None, :]) % (2 * n)) * theta
    ang_b = ((np.arange(bs, dtype=np.int64)[:, None] * t[None, :]) % (2 * n)) * theta
    return [jnp.asarray(f(a), F32) for a in (ang_a, ang_b) for f in (np.cos, np.sin)]


def _twiddles(L):
    ang = (math.pi / L) * np.arange(L // 2, dtype=np.float64)[:, None] * np.ones((1, HYENA_WIDTH))
    return jnp.asarray(np.cos(ang), F32), jnp.asarray(np.sin(ang), F32)


def _spectrum_kernel(feat_ref, w1_ref, b1_ref, f1_ref, w2_ref, b2_ref, f2_ref, w3_ref, b3_ref,
                     decay_ref, tn_ref, ca_ref, sa_ref, cb_ref, sb_ref, twc_ref, tws_ref,
                     hlr_ref, hli_ref, hhr_ref, hhi_ref, hm_ref, cos_ref, sin_ref, opc_s, ops_s, *, L, tk):
    i = pl.program_id(0)
    hi_prec = lax.Precision.HIGHEST
    C = HYENA_WIDTH
    half = L // 2

    bs = DFT_FINE_ROWS
    cb, sb = cb_ref[...], sb_ref[...]
    for j in range(tk // bs):
        ca = ca_ref[pl.ds(i * (tk // bs) + j, 1), :]
        sa = sa_ref[pl.ds(i * (tk // bs) + j, 1), :]
        cos_ref[j * bs:(j + 1) * bs, :] = (ca * cb - sa * sb).astype(BF16)
        sin_ref[j * bs:(j + 1) * bs, :] = (sa * cb + ca * sb).astype(BF16)

    @pl.when(i == 0)
    def _():
        h = jnp.dot(w1_ref[...], feat_ref[...], precision=hi_prec, preferred_element_type=F32)
        h = jnp.sin(f1_ref[...] * (h + b1_ref[...]))
        h = jnp.dot(w2_ref[...], h, precision=hi_prec, preferred_element_type=F32)
        h = jnp.sin(f2_ref[...] * (h + b2_ref[...]))
        h = _dot(h.T.astype(BF16), w3_ref[...].astype(BF16)) + b3_ref[...]
        h = h * jnp.exp(-tn_ref[...] * jnp.abs(decay_ref[...]))
        fwd = h[:, :C]
        bwd = h[:, C:]
        row = lax.broadcasted_iota(jnp.int32, (L, C), 0)
        bwd = jnp.where(row == 0, 0.0, bwd)
        a = fwd + bwd
        b = fwd - bwd
        a_e, a_o, b_e, b_o = a[:half], a[half:], b[:half], b[half:]
        opc_s[...] = jnp.concatenate([a_e, a_o, b_o], axis=1).astype(BF16)
        ops_s[...] = jnp.concatenate([a_o, b_e, b_o], axis=1).astype(BF16)
        sign = jnp.where((lax.broadcasted_iota(jnp.int32, (half, C), 0) & 1) == 0, 1.0, -1.0)
        mid_r = jnp.sum(a_e * sign, axis=0, keepdims=True)
        mid_i = -jnp.sum(b_o * sign, axis=0, keepdims=True)
        hm_ref[...] = jnp.concatenate([mid_r, mid_i, jnp.zeros((6, C), F32)], axis=0) * (1.0 / L)

    pc = _dot(cos_ref[...], opc_s[...])
    ps = _dot(sin_ref[...], ops_s[...])
    ae_c, ao_c, bo_c = pc[:, :C], pc[:, C:2 * C], pc[:, 2 * C:]
    ao_s, be_s, bo_s = ps[:, :C], ps[:, C:2 * C], ps[:, 2 * C:]
    tw_c, tw_s = twc_ref[...], tws_ref[...]
    t_r = tw_c * ao_c - tw_s * ao_s
    t_i = tw_c * bo_s + tw_s * bo_c
    krow = lax.broadcasted_iota(jnp.int32, (tk, C), 0) + i * tk
    wk = jnp.where(krow == 0, 1.0, 2.0) * (1.0 / (2 * L))
    hlr_ref[...] = (ae_c + t_r) * wk
    hhr_ref[...] = (ae_c - t_r) * wk
    hli_ref[...] = (be_s + t_i) * (-wk)
    hhi_ref[...] = (be_s - t_i) * wk


def _spectrum(L, feat, tnorm, twiddles, w1, b1, f1, w2, b2, f2, w3, b3, decay):
    half = L // 2
    tk = min(half, 512)
    C = HYENA_WIDTH
    small = [feat, w1, b1, f1, w2, b2, f2, w3, b3, decay, tnorm] + _dft_seed_tables(half)
    blk = lambda w: pl.BlockSpec((tk, w), lambda i: (i, 0))
    return pl.pallas_call(
        functools.partial(_spectrum_kernel, L=L, tk=tk),
        grid=(half // tk,),
        in_specs=[_const_spec(a.shape) for a in small] + [blk(C), blk(C)],
        out_specs=[blk(C), blk(C), blk(C), blk(C), pl.BlockSpec((8, C), lambda i: (0, 0)), blk(half), blk(half)],
        out_shape=[jax.ShapeDtypeStruct((half, C), F32)] * 4
                  + [jax.ShapeDtypeStruct((8, C), F32),
                     jax.ShapeDtypeStruct((half, half), BF16), jax.ShapeDtypeStruct((half, half), BF16)],
        scratch_shapes=[pltpu.VMEM((half, 3 * C), BF16), pltpu.VMEM((half, 3 * C), BF16)],
        compiler_params=_params("arbitrary"),
        name=f"spectrum_{L}",
    )(*small, *twiddles)


def _modulated_norm(x, g, shift, scale):
    return _rmsnorm(x, g) * (1.0 + scale) + shift


def _rope(x, cos, sin_signed):
    lane = lax.broadcasted_iota(jnp.int32, x.shape, 1)
    swapped = jnp.where((lane & 63) < 32,
                        pltpu.roll(x, HEAD_DIM - 32, axis=1),
                        pltpu.roll(x, 32, axis=1))
    return x * cos + swapped * sin_signed


def _inproj_kernel(*refs, rope, convert_w):
    refs = list(refs)
    wb_ref = refs.pop() if convert_w else None
    if rope:
        x_ref, mod_ref, g_ref, w_ref, cos_ref, sin_ref, q_ref, k_ref, v_ref, vt_ref, hy_ref = refs
    else:
        x_ref, mod_ref, g_ref, w_ref, q_ref, k_ref, v_ref, vt_ref, hy_ref = refs
    if convert_w:
        @pl.when(pl.program_id(0) == 0)
        def _():
            wb_ref[...] = w_ref[...].astype(BF16)
        w_ref = wb_ref
    m = mod_ref[0]
    h = _modulated_norm(x_ref[...], g_ref[...], m[:, 0:D_MODEL], m[:, D_MODEL:2 * D_MODEL]).astype(BF16)

    def heads(y, out_ref, n, scale):
        if not rope:
            out_ref[...] = (y if scale == 1.0 else y * scale).astype(out_ref.dtype)
            return
        cos, sin = cos_ref[...], sin_ref[...]
        if scale != 1.0:
            cos, sin = cos * scale, sin * scale
        for hd in range(n):
            cols = slice(hd * HEAD_DIM, (hd + 1) * HEAD_DIM)
            out_ref[:, cols] = _rope(y[:, cols], cos, sin).astype(out_ref.dtype)

    heads(_dot(h, w_ref[:, :ATTN_WIDTH]), q_ref, N_HEADS, ATTN_SCALE * LOG2E)
    kv = _dot(h, w_ref[:, ATTN_WIDTH:ATTN_WIDTH + 2 * KV_WIDTH])
    heads(kv[:, :KV_WIDTH], k_ref, N_KV_HEADS, 1.0)
    v = kv[:, KV_WIDTH:]
    v_ref[...] = v.astype(v_ref.dtype)
    vt_ref[...] = v.T.astype(vt_ref.dtype)
    hy_ref[...] = _dot(h, w_ref[:, ATTN_WIDTH + 2 * KV_WIDTH:]).astype(hy_ref.dtype)


def _inproj(x, mod, g_pre, w_in, L, rope_tabs, kv_dtype, casts=()):
    R = x.shape[0]
    tm = ROW_TILE
    steps = R // tm
    steps_per_mod = (R // mod.shape[0]) // tm
    convert_w = w_in.dtype != BF16
    row = lambda i: (i, 0)
    in_specs = [pl.BlockSpec((tm, D_MODEL), row),
                pl.BlockSpec((1, 1, mod.shape[2]), lambda i: (i // steps_per_mod, 0, 0)),
                _const_spec(g_pre.shape), _const_spec(w_in.shape)]
    args = [x, mod, g_pre, w_in]
    if rope_tabs is not None:
        steps_per_seq = L // tm
        tab = pl.BlockSpec((tm, HEAD_DIM), lambda i: (i % steps_per_seq, 0))
        in_specs += [tab, tab]
        args += list(rope_tabs)
    out_specs = [pl.BlockSpec((tm, ATTN_WIDTH), row), pl.BlockSpec((tm, KV_WIDTH), row),
                 pl.BlockSpec((tm, KV_WIDTH), row), pl.BlockSpec((KV_WIDTH, tm), lambda i: (0, i)),
                 pl.BlockSpec((tm, 3 * HYENA_WIDTH), row)]
    out_shape = [jax.ShapeDtypeStruct((R, ATTN_WIDTH), BF16), jax.ShapeDtypeStruct((R, KV_WIDTH), kv_dtype),
                 jax.ShapeDtypeStruct((R, KV_WIDTH), kv_dtype), jax.ShapeDtypeStruct((KV_WIDTH, R), BF16),
                 jax.ShapeDtypeStruct((R, 3 * HYENA_WIDTH), BF16)]
    if convert_w:
        out_specs.append(pl.BlockSpec(w_in.shape, lambda i: (0, 0)))
        out_shape.append(jax.ShapeDtypeStruct(w_in.shape, BF16))
    cast_specs, cast_shapes = _cast_plan(casts, steps, lambda i: i)
    body = functools.partial(_inproj_kernel, rope=rope_tabs is not None, convert_w=convert_w)
    return pl.pallas_call(
        _with_casts(body, len(args), len(out_specs), len(casts)),
        grid=(steps,),
        in_specs=in_specs + cast_specs,
        out_specs=out_specs + cast_specs,
        out_shape=out_shape + cast_shapes,
        compiler_params=_params("arbitrary"),
        name=f"inproj_{L}",
    )(*args, *casts)


def _rope_tables(L):
    rows = L // GRID_W
    row_ids = np.repeat(np.arange(rows), GRID_W)
    col_ids = np.tile(np.arange(GRID_W), rows)
    half = HEAD_DIM // 2
    inv_freq = ROPE_THETA ** (-np.arange(0, half, 2, dtype=np.float64) / half)
    cos_parts, sin_parts = [], []
    for pos in (row_ids, col_ids):
        ang = pos.astype(np.float64)[:, None] * inv_freq[None, :]
        cos_parts += [np.cos(ang), np.cos(ang)]
        sin_parts += [-np.sin(ang), np.sin(ang)]
    return (jnp.asarray(np.concatenate(cos_parts, axis=-1), F32),
            jnp.asarray(np.concatenate(sin_parts, axis=-1), F32))


def _band_bias():
    key = np.arange(ATTN_KEY_SPAN)[:, None]
    qry = (np.arange(GROUP * ATTN_Q_TILE) % ATTN_Q_TILE)[None, :]
    tabs = [np.where(np.abs(key - off - qry) <= WINDOW, 0.0, NEG_INF) for off in (0, WINDOW, 2 * WINDOW)]
    return jnp.asarray(np.stack(tabs), F32)


def _attn_kernel(*refs, windowed, L, nblk):
    if windowed:
        sink_ref, q_ref, k_ref, vt_ref, ck_ref, cvt_ref, bias_ref, o_ref, s_scr, p_scr, d_scr = refs
    else:
        sink_ref, q_ref, k_ref, vt_ref, o_ref, s_scr, p_scr, d_scr = refs
    tq = ATTN_Q_TILE
    ncol = GROUP * tq
    nkeys = s_scr.shape[1]
    nt = (((1,), (1,)), ((), ()))
    pairs = [(jb, kk) for jb in range(nblk) for kk in range(N_KV_HEADS)]

    def place(c):
        jb, kk = pairs[c]
        rows = slice(jb * tq, (jb + 1) * tq)
        kv = slice(kk * HEAD_DIM, (kk + 1) * HEAD_DIM)
        heads = [kk * GROUP + g for g in range(GROUP)]
        qs = (pl.program_id(1) * nblk + jb) * tq
        start = pl.multiple_of(jnp.clip(qs - WINDOW, 0, L - ATTN_KEY_SPAN), WINDOW) if windowed else 0
        return rows, kv, heads, qs, start

    def scores(c):
        rows, kv, heads, qs, start = place(c)
        q4 = jnp.concatenate([q_ref[rows, hd * HEAD_DIM:(hd + 1) * HEAD_DIM] for hd in heads], axis=0)
        if windowed:
            s_scr[c % 2, :ATTN_KEY_SPAN] = lax.dot_general(
                k_ref[pl.ds(start, ATTN_KEY_SPAN), kv], q4, nt,
                preferred_element_type=F32) + bias_ref[(qs - start) // WINDOW]
            s_scr[c % 2, ATTN_KEY_SPAN:] = lax.dot_general(
                ck_ref[:, kv].astype(BF16), q4, nt, preferred_element_type=F32)
        else:
            s_scr[c % 2] = lax.dot_general(k_ref[:, kv].astype(BF16), q4, nt, preferred_element_type=F32)

    def fold(acc, blk, op):
        for r in range(0, ATTN_KEY_CHUNK, 8):
            acc = op(acc, blk[r:r + 8])
        return acc

    def softmax(c):
        _, _, heads, _, _ = place(c)
        slot = c % 2
        sink = jnp.concatenate([jnp.full((1, tq), sink_ref[hd] * LOG2E, F32) for hd in heads], axis=1)
        m8 = jnp.full((8, ncol), NEG_INF, F32)
        for r in range(0, nkeys, ATTN_KEY_CHUNK):
            m8 = fold(m8, s_scr[slot, r:r + ATTN_KEY_CHUNK, :], jnp.maximum)
        m = jnp.maximum(jnp.max(m8, axis=0, keepdims=True), sink)
        l8 = jnp.zeros((8, ncol), F32)
        for r in range(0, nkeys, ATTN_KEY_CHUNK):
            p = jnp.exp2(s_scr[slot, r:r + ATTN_KEY_CHUNK, :] - m)
            l8 = fold(l8, p, jnp.add)
            p_scr[slot, r:r + ATTN_KEY_CHUNK, :] = p.astype(BF16)
        denom = jnp.sum(l8, axis=0, keepdims=True) + jnp.exp2(sink - m)
        d_scr[slot] = jnp.broadcast_to(1.0 / denom, (8, ncol))

    def values(c):
        rows, kv, heads, _, start = place(c)
        slot = c % 2
        if windowed:
            o = (_dot(vt_ref[kv, pl.ds(start, ATTN_KEY_SPAN)], p_scr[slot, :ATTN_KEY_SPAN])
                 + _dot(cvt_ref[kv, :].astype(BF16), p_scr[slot, ATTN_KEY_SPAN:]))
        else:
            o = _dot(vt_ref[kv, :], p_scr[slot])
        o = (o * d_scr[slot, 0:1, :]).T
        for g, hd in enumerate(heads):
            o_ref[rows, hd * HEAD_DIM:(hd + 1) * HEAD_DIM] = o[g * tq:(g + 1) * tq].astype(o_ref.dtype)

    n = len(pairs)
    scores(0)
    scores(1)
    softmax(0)
    for c in range(n):
        if c + 2 < n:
            scores(c + 2)
        if c + 1 < n:
            softmax(c + 1)
        values(c)


def _attention(q, k, vt, sink, B, L, ctx_k=None, ctx_vt=None, casts=()):
    windowed = ctx_k is not None
    nblk = min(L // ATTN_Q_TILE, 4)
    tstep = nblk * ATTN_Q_TILE
    nq = L // tstep
    sink_spec = pl.BlockSpec(memory_space=pltpu.SMEM)
    q_spec = pl.BlockSpec((tstep, ATTN_WIDTH), lambda b, i: (b * nq + i, 0))
    in_specs = [sink_spec, q_spec,
                pl.BlockSpec((L, KV_WIDTH), lambda b, i: (b, 0)),
                pl.BlockSpec((KV_WIDTH, L), lambda b, i: (0, b))]
    args = [sink, q, k, vt]
    nkeys, ncol = L, GROUP * ATTN_Q_TILE
    if windowed:
        assert L >= ATTN_KEY_SPAN + WINDOW
        P = ctx_k.shape[0] // B
        nkeys = ATTN_KEY_SPAN + P
        bias = _band_bias()
        in_specs += [pl.BlockSpec((P, KV_WIDTH), lambda b, i: (b, 0)),
                     pl.BlockSpec((KV_WIDTH, P), lambda b, i: (b, 0)),
                     _const_spec(bias.shape)]
        args += [ctx_k, ctx_vt, bias]
    cast_specs, cast_shapes = _cast_plan(casts, B * nq, lambda b, i: b * nq + i)
    body = functools.partial(_attn_kernel, windowed=windowed, L=L, nblk=nblk)
    out = pl.pallas_call(
        _with_casts(body, len(args), 1, len(casts)),
        grid=(B, nq),
        in_specs=in_specs + cast_specs,
        out_specs=[q_spec] + cast_specs,
        out_shape=[jax.ShapeDtypeStruct((B * L, ATTN_WIDTH), BF16)] + cast_shapes,
        scratch_shapes=[pltpu.VMEM((2, nkeys, ncol), F32), pltpu.VMEM((2, nkeys, ncol), BF16),
                        pltpu.VMEM((2, 8, ncol), F32)],
        compiler_params=_params("arbitrary", "arbitrary"),
        name=f"attention_{L}",
    )(*args, *casts)
    return out if casts else out[0]


def _hyena_kernel(x0_ref, x1_ref, v_ref, cw0_ref, cw1_ref, cwv_ref, cb0_ref, cb1_ref, cbv_ref,
                  hlr_ref, hli_ref, hhr_ref, hhi_ref, hm_ref, twc_ref, tws_ref, skip_ref,
                  cos_ref, sin_ref, o_ref, z_scr, y_scr, *, L, nseq):
    rows, tc = o_ref.shape
    half = L // 2
    pos = lax.broadcasted_iota(jnp.int32, (rows, tc), 0) & (L - 1)
    first = pos == 0
    last = pos == L - 1

    def short_conv(u_ref, w_ref, b_ref):
        u = u_ref[...].astype(F32)
        prev = jnp.where(first, 0.0, pltpu.roll(u, 1, axis=0))
        nxt = jnp.where(last, 0.0, pltpu.roll(u, rows - 1, axis=0))
        return prev * w_ref[0:1, :] + u * w_ref[1:2, :] + nxt * w_ref[2:3, :] + b_ref[...]

    def cmul(ar, ai, br, bi):
        return ar * br - ai * bi, ar * bi + ai * br

    lanes = range(tc // LANES)

    def deinterleave(scr, start):
        return jnp.concatenate([scr[j, pl.ds(start, half, stride=2), :] for j in lanes], axis=1)

    z = short_conv(v_ref, cwv_ref, cbv_ref) * short_conv(x1_ref, cw1_ref, cb1_ref)
    for j in lanes:
        z_scr[j] = z[:, j * LANES:(j + 1) * LANES]
    parts = [deinterleave(z_scr, s * L + par) for s in range(nseq) for par in (0, 1)]
    zcat = jnp.concatenate(parts, axis=1).astype(BF16)
    fc = _dot(cos_ref[...], zcat)
    fs = _dot(sin_ref[...], zcat)
    tw_c, tw_s = twc_ref[...], tws_ref[...]
    hlr, hli, hhr, hhi = hlr_ref[...], hli_ref[...], hhr_ref[...], hhi_ref[...]
    sign = jnp.where((lax.broadcasted_iota(jnp.int32, (half, tc), 0) & 1) == 0, 1.0, -1.0)
    g_r, g_i, mids = [], [], []
    for s in range(nseq):
        ev = slice(s * 2 * tc, s * 2 * tc + tc)
        od = slice(s * 2 * tc + tc, (s + 1) * 2 * tc)
        t_r, t_i = cmul(tw_c, -tw_s, fc[:, od], -fs[:, od])
        zl_r, zl_i = fc[:, ev] + t_r, t_i - fs[:, ev]
        zh_r, zh_i = fc[:, ev] - t_r, fs[:, ev] + t_i
        wl_r, wl_i = cmul(zl_r, zl_i, hlr, hli)
        wh_r, wh_i = cmul(zh_r, zh_i, hhr, hhi)
        g1_r, g1_i = cmul(wl_r - wh_r, wl_i + wh_i, tw_c, tw_s)
        g_r += [wl_r + wh_r, g1_r]
        g_i += [wl_i - wh_i, g1_i]
        e_m = jnp.sum(parts[2 * s] * sign, axis=0, keepdims=True)
        o_m = jnp.sum(parts[2 * s + 1] * sign, axis=0, keepdims=True)
        mids.append(cmul(e_m, -o_m, hm_ref[0:1, :], hm_ref[1:2, :]))
    y2 = (_dot(cos_ref[...], jnp.concatenate(g_r, axis=1).astype(BF16))
          - _dot(sin_ref[...], jnp.concatenate(g_i, axis=1).astype(BF16)))
    for s in range(nseq):
        wm_r, wm_i = mids[s]
        y_e = y2[:, s * 2 * tc:s * 2 * tc + tc] + sign * wm_r
        y_o = y2[:, s * 2 * tc + tc:(s + 1) * 2 * tc] - sign * wm_i
        for j in lanes:
            y_scr[j, pl.ds(s * L, half, stride=2), :] = y_e[:, j * LANES:(j + 1) * LANES]
            y_scr[j, pl.ds(s * L + 1, half, stride=2), :] = y_o[:, j * LANES:(j + 1) * LANES]
    y = jnp.concatenate([y_scr[j] for j in lanes], axis=1) + z * skip_ref[...]
    o_ref[...] = (short_conv(x0_ref, cw0_ref, cb0_ref) * y).astype(o_ref.dtype)


def _hyena(hy, conv_w, conv_b, spectrum, twiddles, skip, B, L, casts=()):
    hlr, hli, hhr, hhi, hm, cos_t, sin_t = spectrum
    C = HYENA_WIDTH
    tc = HYENA_CH_TILE
    nj = C // tc
    half = L // 2
    assert L & (L - 1) == 0
    nseq = min(B, max(1, HYENA_ROWS_PER_STEP // L))
    seq = nseq * L

    def part(p, rows):
        return pl.BlockSpec((rows, tc), lambda b, j: (b if rows == seq else 0, p * nj + j))

    spec_c = pl.BlockSpec((half, tc), lambda b, j: (0, j))
    conv_b = conv_b.reshape(1, 3 * C)
    args = [hy, hy, hy, conv_w, conv_w, conv_w, conv_b, conv_b, conv_b,
            hlr, hli, hhr, hhi, hm, *twiddles, skip.reshape(1, C), cos_t, sin_t]
    cast_specs, cast_shapes = _cast_plan(casts, (B // nseq) * nj, lambda b, j: b * nj + j)
    out = pl.pallas_call(
        _with_casts(functools.partial(_hyena_kernel, L=L, nseq=nseq), len(args), 1, len(casts)),
        grid=(B // nseq, nj),
        in_specs=[part(0, seq), part(1, seq), part(2, seq),
                  part(0, 3), part(1, 3), part(2, 3),
                  part(0, 1), part(1, 1), part(2, 1),
                  spec_c, spec_c, spec_c, spec_c, pl.BlockSpec((8, tc), lambda b, j: (0, j)),
                  spec_c, spec_c, pl.BlockSpec((1, tc), lambda b, j: (0, j)),
                  _const_spec((half, half)), _const_spec((half, half))] + cast_specs,
        out_specs=[pl.BlockSpec((seq, tc), lambda b, j: (b, j))] + cast_specs,
        out_shape=[jax.ShapeDtypeStruct((B * L, C), BF16)] + cast_shapes,
        scratch_shapes=[pltpu.VMEM((tc // LANES, seq, LANES), F32)] * 2,
        compiler_params=_params("arbitrary", "arbitrary"),
        name=f"hyena_{L}",
    )(*args, *casts)
    return out if casts else out[0]


def _merge_ffn_kernel(x_ref, attn_ref, hyo_ref, mod_ref, g_pre_ref, g_post_ref, g_fpre_ref, g_fpost_ref,
                      wg_ref, bg_ref, wpa_ref, wph_ref, wo_ref, wup_ref, wdn_ref, o_ref):
    D = D_MODEL
    m = mod_ref[0]
    sh1, sc1, g1, sh2, sc2, g2 = [m[:, j * D:(j + 1) * D] for j in range(6)]
    x = x_ref[...]
    h = _modulated_norm(x, g_pre_ref[...], sh1, sc1).astype(BF16)
    gate_a = jax.nn.sigmoid(_dot(h, wg_ref[:, :D]) + bg_ref[:, :D])
    merged = gate_a * _dot(attn_ref[...], wpa_ref[...])
    gate_h = jax.nn.sigmoid(_dot(h, wg_ref[:, D:]) + bg_ref[:, D:])
    merged = merged + gate_h * _dot(hyo_ref[...], wph_ref[...])
    mix = _dot(merged.astype(BF16), wo_ref[...])
    x = x + g1 * _rmsnorm(mix, g_post_ref[...])
    h2 = _modulated_norm(x, g_fpre_ref[...], sh2, sc2).astype(BF16)
    f = jnp.zeros(x.shape, F32)
    for c0 in range(0, D_FF, FFN_CHUNK):
        gt = _dot(h2, wup_ref[:, c0:c0 + FFN_CHUNK])
        up = _dot(h2, wup_ref[:, D_FF + c0:D_FF + c0 + FFN_CHUNK])
        f = f + _dot((_silu(gt) * up).astype(BF16), wdn_ref[c0:c0 + FFN_CHUNK, :])
    o_ref[...] = x + g2 * _rmsnorm(f, g_fpost_ref[...])


def _merge_ffn(x, attn, hyo, mod, norms, w_gate, b_gate, w_pa, w_ph, w_o, w_up, w_down, L):
    R = x.shape[0]
    tm = ROW_TILE
    steps_per_mod = (R // mod.shape[0]) // tm
    row = lambda i: (i, 0)
    consts = list(norms) + [w_gate, b_gate, w_pa, w_ph, w_o, w_up, w_down]
    return pl.pallas_call(
        _merge_ffn_kernel,
        grid=(R // tm,),
        in_specs=[pl.BlockSpec((tm, D_MODEL), row), pl.BlockSpec((tm, ATTN_WIDTH), row),
                  pl.BlockSpec((tm, HYENA_WIDTH), row),
                  pl.BlockSpec((1, 1, mod.shape[2]), lambda i: (i // steps_per_mod, 0, 0))]
                 + [_const_spec(a.shape) for a in consts],
        out_specs=pl.BlockSpec((tm, D_MODEL), row),
        out_shape=jax.ShapeDtypeStruct((R, D_MODEL), F32),
        compiler_params=_params("arbitrary"),
        name=f"merge_ffn_{L}",
    )(x, attn, hyo, mod, *consts)


def _filter_features(L):
    t = np.concatenate([np.arange(0, L, 2), np.arange(1, L, 2)]).astype(np.float64)
    t_norm = t / L
    bands = np.linspace(1e-4, FILTER_BANDS - 1, FILTER_BANDS)
    ang = (2.0 * math.pi * t / L)[:, None] * bands[None, :]
    feat = np.concatenate([t_norm[:, None], np.cos(ang), np.sin(ang)], axis=-1)
    feat = np.pad(feat, ((0, 0), (0, FILTER_FEAT_PAD - FILTER_FEAT)))
    return jnp.asarray(feat.T, F32), jnp.asarray(t_norm[:, None], F32)


def _layer(x, mod, p, w, B, L, latent, ctx_k, ctx_v):
    convert = w['w_in'].dtype != BF16
    side = (lambda *names: [w[n] for n in names]) if convert else (lambda *names: [])
    wb = dict(w)
    feat, tnorm = _filter_features(L)
    twiddles = _twiddles(L)
    spectrum = _spectrum(L, feat, tnorm, twiddles, p['filt_w1'], p['filt_b1'], p['filt_freq1'],
                         p['filt_w2'], p['filt_b2'], p['filt_freq2'], p['filt_w3'], p['filt_b3'],
                         p['filt_decay'])
    rope_tabs = _rope_tables(L) if latent else None
    q, k, v, vt, hy, *converted = _inproj(x, mod, p['norm_mix_pre'], w['w_in'], L, rope_tabs,
                                          BF16 if latent else F32, side('w_pa', 'w_o', 'w_ph'))
    if convert:
        wb['w_in'], wb['w_pa'], wb['w_o'], wb['w_ph'] = converted
    attn = _attention(q, k, vt, p['attn_sink'], B, L, ctx_k, ctx_v, side('w_up'))
    hyo = _hyena(hy, p['conv_w'], p['conv_b'], spectrum, twiddles, p['hyena_skip'], B, L,
                 side('w_gate', 'w_down'))
    if convert:
        attn, wb['w_up'] = attn
        hyo, wb['w_gate'], wb['w_down'] = hyo
    norms = [p['norm_mix_pre'], p['norm_mix_post'], p['norm_ffn_pre'], p['norm_ffn_post']]
    y = _merge_ffn(x, attn, hyo, mod, norms, wb['w_gate'], p['b_gate'], wb['w_pa'], wb['w_ph'], wb['w_o'],
                   wb['w_up'], wb['w_down'], L)
    return y, k, v, wb


def kernel(x_prompt, x_sample, c, cache_k, cache_v, c_ctx, norm_mix_pre, norm_mix_post, norm_ffn_pre, norm_ffn_post, w_mod, b_mod, w_in, attn_sink, conv_w, conv_b, filt_w1, filt_b1, filt_freq1, filt_w2, filt_b2, filt_freq2, filt_w3, filt_b3, filt_decay, hyena_skip, w_pa, w_ph, w_gate, b_gate, w_o, w_up, w_down):
    Bp, Lp, D = x_prompt.shape
    Bs, Ls, _ = x_sample.shape
    depth = w_in.shape[0]
    past = cache_k.shape[2]
    yp = x_prompt.reshape(Bp * Lp, D)
    ys = x_sample.reshape(Bs * Ls, D)
    cond = jnp.concatenate([c_ctx[None, :], c, jnp.zeros((8 - 1 - Bs, D), F32)], axis=0)
    new_k, new_v = [], []
    row2 = lambda a: a.reshape(1, -1)
    col2 = lambda a: a.reshape(-1, 1)
    for l in range(depth):
        w1 = jnp.pad(filt_w1[l], ((0, FILTER_FEAT_PAD - FILTER_FEAT), (0, 0)))
        p = {
            'norm_mix_pre': row2(norm_mix_pre[l]), 'norm_mix_post': row2(norm_mix_post[l]),
            'norm_ffn_pre': row2(norm_ffn_pre[l]), 'norm_ffn_post': row2(norm_ffn_post[l]),
            'attn_sink': attn_sink[l], 'conv_w': conv_w[l], 'conv_b': conv_b[l],
            'filt_w1': w1.T, 'filt_b1': col2(filt_b1[l]), 'filt_freq1': col2(filt_freq1[l]),
            'filt_w2': filt_w2[l].T, 'filt_b2': col2(filt_b2[l]), 'filt_freq2': col2(filt_freq2[l]),
            'filt_w3': filt_w3[l], 'filt_b3': row2(filt_b3[l]), 'filt_decay': row2(filt_decay[l]),
            'hyena_skip': hyena_skip[l], 'b_gate': row2(b_gate[l]),
        }
        w = {'w_in': w_in[l], 'w_gate': w_gate[l], 'w_pa': w_pa[l], 'w_ph': w_ph[l], 'w_o': w_o[l],
             'w_up': w_up[l], 'w_down': w_down[l]}
        mod = _modulation(cond, w_mod[l], b_mod[l])
        mod_p = mod[0:1].reshape(1, 1, 6 * D)
        mod_s = mod[1:1 + Bs].reshape(Bs, 1, 6 * D)
        yp, k_ctx, v_ctx, w = _layer(yp, mod_p, p, w, Bp, Lp, False, None, None)
        new_k.append(k_ctx.reshape(Bp, Lp, N_KV_HEADS, HEAD_DIM))
        new_v.append(v_ctx.reshape(Bp, Lp, N_KV_HEADS, HEAD_DIM))
        ck = cache_k[:, l].reshape(Bs * past, KV_WIDTH)
        cv = jnp.swapaxes(cache_v[:, l].reshape(Bs, past, KV_WIDTH), 1, 2).reshape(Bs * KV_WIDTH, past)
        ys, _, _, _ = _layer(ys, mod_s, p, w, Bs, Ls, True, ck, cv)
    return (yp.reshape(Bp, Lp, D), ys.reshape(Bs, Ls, D),
            jnp.stack(new_k, axis=1), jnp.stack(new_v, axis=1))
```

```python
import functools
import math

import numpy as np
import jax
import jax.numpy as jnp
from jax import lax
from jax.experimental import pallas as pl
from jax.experimental.pallas import tpu as pltpu

D_MODEL = 1024
GRID_W = 64
N_HEADS = 8
N_KV_HEADS = 2
HEAD_DIM = 128
GROUP = N_HEADS // N_KV_HEADS
ATTN_WIDTH = N_HEADS * HEAD_DIM
KV_WIDTH = N_KV_HEADS * HEAD_DIM
WINDOW = 128
ATTN_SCALE = HEAD_DIM ** -0.5
LOG2E = math.log2(math.e)
ROPE_THETA = 10000.0
HYENA_WIDTH = D_MODEL // 2
FILTER_BANDS = 8
FILTER_FEAT = 1 + 2 * FILTER_BANDS
FILTER_FEAT_PAD = 32
FILTER_HIDDEN = 64
D_FF = -(-8 * D_MODEL // (3 * 256)) * 256
IN_WIDTH = ATTN_WIDTH + 2 * KV_WIDTH + 3 * HYENA_WIDTH
EPS = 1e-6
NEG_INF = -1e30

V7X_VMEM_BYTES = 64 * 1024 * 1024
VMEM_LIMIT = V7X_VMEM_BYTES - 8 * 1024 * 1024
MXU_COLS = 256
LANES = 128
FFN_CHUNK = MXU_COLS
ROW_TILE = 512
ATTN_KEY_CHUNK = 64
ATTN_Q_TILE = 128
ATTN_KEY_SPAN = ATTN_Q_TILE + 2 * WINDOW
DFT_FINE_ROWS = 32
HYENA_CH_TILE = 256
HYENA_ROWS_PER_STEP = 2048

BF16 = jnp.bfloat16
F32 = jnp.float32


def _params(*semantics):
    return pltpu.CompilerParams(dimension_semantics=semantics, vmem_limit_bytes=VMEM_LIMIT)


def _const_spec(shape):
    zeros = (0,) * len(shape)
    return pl.BlockSpec(shape, lambda *_: zeros, pipeline_mode=pl.Buffered(1))


def _dot(a, b):
    return jnp.dot(a, b, preferred_element_type=F32)


def _with_casts(body, n_in, n_out, ncast):
    if ncast == 0:
        return body

    def kernel(*refs):
        ins, rest = refs[:n_in], refs[n_in:]
        cast_in, rest = rest[:ncast], rest[ncast:]
        outs, rest = rest[:n_out], rest[n_out:]
        cast_out, scratch = rest[:ncast], rest[ncast:]
        for src, dst in zip(cast_in, cast_out):
            dst[...] = src[...].astype(dst.dtype)
        body(*ins, *outs, *scratch)

    return kernel


def _cast_plan(casts, nsteps, linear_step):
    specs, shapes = [], []
    for a in casts:
        rows, cols = a.shape
        slab = rows // nsteps
        assert slab * nsteps == rows and slab % 16 == 0, (a.shape, nsteps)
        specs.append(pl.BlockSpec((slab, cols), lambda *g: (linear_step(*g), 0)))
        shapes.append(jax.ShapeDtypeStruct(a.shape, BF16))
    return specs, shapes


def _rmsnorm(x, g):
    return x * lax.rsqrt(jnp.mean(x * x, axis=-1, keepdims=True) + EPS) * g


def _silu(x):
    return x * jax.nn.sigmoid(x)


def _mod_kernel(cond_ref, w_ref, b_ref, o_ref):
    a = _silu(cond_ref[...]).astype(BF16)
    o_ref[...] = _dot(a, w_ref[...].astype(BF16)) + b_ref[...]


def _modulation(cond, w_mod, b_mod):
    rows = cond.shape[0]
    n = w_mod.shape[1]
    tn = D_MODEL
    return pl.pallas_call(
        _mod_kernel,
        grid=(n // tn,),
        in_specs=[pl.BlockSpec((rows, D_MODEL), lambda j: (0, 0)),
                  pl.BlockSpec((D_MODEL, tn), lambda j: (0, j)),
                  pl.BlockSpec((1, tn), lambda j: (0, j))],
        out_specs=pl.BlockSpec((rows, tn), lambda j: (0, j)),
        out_shape=jax.ShapeDtypeStruct((rows, n), F32),
        compiler_params=_params("arbitrary"),
        name="modulation",
    )(cond, w_mod, b_mod.reshape(1, n))


def _dft_seed_tables(n):
    bs = DFT_FINE_ROWS
    t = np.arange(n, dtype=np.int64)
    theta = math.pi / n
    ang_a = ((np.arange(n // bs, dtype=np.int64)[:, None] * bs * t[None, :]) % (2 * n)) * theta
    ang_b = ((np.arange(bs, dtype=np.int64)[:, None] * t[None, :]) % (2 * n)) * theta
    return [jnp.asarray(f(a), F32) for a in (ang_a, ang_b) for f in (np.cos, np.sin)]


def _twiddles(L):
    ang = (math.pi / L) * np.arange(L // 2, dtype=np.float64)[:, None] * np.ones((1, HYENA_WIDTH))
    return jnp.asarray(np.cos(ang), F32), jnp.asarray(np.sin(ang), F32)


def _spectrum_kernel(feat_ref, w1_ref, b1_ref, f1_ref, w2_ref, b2_ref, f2_ref, w3_ref, b3_ref,
                     decay_ref, tn_ref, ca_ref, sa_ref, cb_ref, sb_ref, twc_ref, tws_ref,
                     hlr_ref, hli_ref, hhr_ref, hhi_ref, hm_ref, cos_ref, sin_ref, opc_s, ops_s, *, L, tk):
    i = pl.program_id(0)
    hi_prec = lax.Precision.HIGHEST
    C = HYENA_WIDTH
    half = L // 2

    bs = DFT_FINE_ROWS
    cb, sb = cb_ref[...], sb_ref[...]
    for j in range(tk // bs):
        ca = ca_ref[pl.ds(i * (tk // bs) + j, 1), :]
        sa = sa_ref[pl.ds(i * (tk // bs) + j, 1), :]
        cos_ref[j * bs:(j + 1) * bs, :] = (ca * cb - sa * sb).astype(BF16)
        sin_ref[j * bs:(j + 1) * bs, :] = (sa * cb + ca * sb).astype(BF16)

    @pl.when(i == 0)
    def _():
        h = jnp.dot(w1_ref[...], feat_ref[...], precision=hi_prec, preferred_element_type=F32)
        h = jnp.sin(f1_ref[...] * (h + b1_ref[...]))
        h = jnp.dot(w2_ref[...], h, precision=hi_prec, preferred_element_type=F32)
        h = jnp.sin(f2_ref[...] * (h + b2_ref[...]))
        h = _dot(h.T.astype(BF16), w3_ref[...].astype(BF16)) + b3_ref[...]
        h = h * jnp.exp(-tn_ref[...] * jnp.abs(decay_ref[...]))
        fwd = h[:, :C]
        bwd = h[:, C:]
        row = lax.broadcasted_iota(jnp.int32, (L, C), 0)
        bwd = jnp.where(row == 0, 0.0, bwd)
        a = fwd + bwd
        b = fwd - bwd
        a_e, a_o, b_e, b_o = a[:half], a[half:], b[:half], b[half:]
        opc_s[...] = jnp.concatenate([a_e, a_o, b_o], axis=1).astype(BF16)
        ops_s[...] = jnp.concatenate([a_o, b_e, b_o], axis=1).astype(BF16)
        sign = jnp.where((lax.broadcasted_iota(jnp.int32, (half, C), 0) & 1) == 0, 1.0, -1.0)
        mid_r = jnp.sum(a_e * sign, axis=0, keepdims=True)
        mid_i = -jnp.sum(b_o * sign, axis=0, keepdims=True)
        hm_ref[...] = jnp.concatenate([mid_r, mid_i, jnp.zeros((6, C), F32)], axis=0) * (1.0 / L)

    pc = _dot(cos_ref[...], opc_s[...])
    ps = _dot(sin_ref[...], ops_s[...])
    ae_c, ao_c, bo_c = pc[:, :C], pc[:, C:2 * C], pc[:, 2 * C:]
    ao_s, be_s, bo_s = ps[:, :C], ps[:, C:2 * C], ps[:, 2 * C:]
    tw_c, tw_s = twc_ref[...], tws_ref[...]
    t_r = tw_c * ao_c - tw_s * ao_s
    t_i = tw_c * bo_s + tw_s * bo_c
    krow = lax.broadcasted_iota(jnp.int32, (tk, C), 0) + i * tk
    wk = jnp.where(krow == 0, 1.0, 2.0) * (1.0 / (2 * L))
    hlr_ref[...] = (ae_c + t_r) * wk
    hhr_ref[...] = (ae_c - t_r) * wk
    hli_ref[...] = (be_s + t_i) * (-wk)
    hhi_ref[...] = (be_s - t_i) * wk


def _spectrum(L, feat, tnorm, twiddles, w1, b1, f1, w2, b2, f2, w3, b3, decay):
    half = L // 2
    tk = min(half, 512)
    C = HYENA_WIDTH
    small = [feat, w1, b1, f1, w2, b2, f2, w3, b3, decay, tnorm] + _dft_seed_tables(half)
    blk = lambda w: pl.BlockSpec((tk, w), lambda i: (i, 0))
    return pl.pallas_call(
        functools.partial(_spectrum_kernel, L=L, tk=tk),
        grid=(half // tk,),
        in_specs=[_const_spec(a.shape) for a in small] + [blk(C), blk(C)],
        out_specs=[blk(C), blk(C), blk(C), blk(C), pl.BlockSpec((8, C), lambda i: (0, 0)), blk(half), blk(half)],
        out_shape=[jax.ShapeDtypeStruct((half, C), F32)] * 4
                  + [jax.ShapeDtypeStruct((8, C), F32),
                     jax.ShapeDtypeStruct((half, half), BF16), jax.ShapeDtypeStruct((half, half), BF16)],
        scratch_shapes=[pltpu.VMEM((half, 3 * C), BF16), pltpu.VMEM((half, 3 * C), BF16)],
        compiler_params=_params("arbitrary"),
        name=f"spectrum_{L}",
    )(*small, *twiddles)


def _modulated_norm(x, g, shift, scale):
    return _rmsnorm(x, g) * (1.0 + scale) + shift


def _rope(x, cos, sin_signed):
    lane = lax.broadcasted_iota(jnp.int32, x.shape, 1)
    swapped = jnp.where((lane & 63) < 32,
                        pltpu.roll(x, HEAD_DIM - 32, axis=1),
                        pltpu.roll(x, 32, axis=1))
    return x * cos + swapped * sin_signed


def _inproj_kernel(*refs, rope, convert_w):
    refs = list(refs)
    wb_ref = refs.pop() if convert_w else None
    if rope:
        x_ref, mod_ref, g_ref, w_ref, cos_ref, sin_ref, q_ref, k_ref, v_ref, vt_ref, hy_ref = refs
    else:
        x_ref, mod_ref, g_ref, w_ref, q_ref, k_ref, v_ref, vt_ref, hy_ref = refs
    if convert_w:
        @pl.when(pl.program_id(0) == 0)
        def _():
            wb_ref[...] = w_ref[...].astype(BF16)
        w_ref = wb_ref
    m = mod_ref[0]
    h = _modulated_norm(x_ref[...], g_ref[...], m[:, 0:D_MODEL], m[:, D_MODEL:2 * D_MODEL]).astype(BF16)

    def heads(y, out_ref, n, scale):
        if not rope:
            out_ref[...] = (y if scale == 1.0 else y * scale).astype(out_ref.dtype)
            return
        cos, sin = cos_ref[...], sin_ref[...]
        if scale != 1.0:
            cos, sin = cos * scale, sin * scale
        for hd in range(n):
            cols = slice(hd * HEAD_DIM, (hd + 1) * HEAD_DIM)
            out_ref[:, cols] = _rope(y[:, cols], cos, sin).astype(out_ref.dtype)

    heads(_dot(h, w_ref[:, :ATTN_WIDTH]), q_ref, N_HEADS, ATTN_SCALE * LOG2E)
    kv = _dot(h, w_ref[:, ATTN_WIDTH:ATTN_WIDTH + 2 * KV_WIDTH])
    def store_kv(y, out_ref):
        if out_ref.shape[1] == KV_WIDTH:
            out_ref[...] = y.astype(out_ref.dtype)
            return
        for hd in range(N_KV_HEADS):
            out_ref[pl.ds(hd, y.shape[0], stride=N_KV_HEADS), :] = (
                y[:, hd * HEAD_DIM:(hd + 1) * HEAD_DIM].astype(out_ref.dtype))

    if rope:
        heads(kv[:, :KV_WIDTH], k_ref, N_KV_HEADS, 1.0)
    else:
        store_kv(kv[:, :KV_WIDTH], k_ref)
    v = kv[:, KV_WIDTH:]
    store_kv(v, v_ref)
    vt_ref[...] = v.T.astype(vt_ref.dtype)
    hy_ref[...] = _dot(h, w_ref[:, ATTN_WIDTH + 2 * KV_WIDTH:]).astype(hy_ref.dtype)


def _inproj(x, mod, g_pre, w_in, L, rope_tabs, kv_dtype, casts=()):
    R = x.shape[0]
    tm = ROW_TILE
    steps = R // tm
    steps_per_mod = (R // mod.shape[0]) // tm
    convert_w = w_in.dtype != BF16
    row = lambda i: (i, 0)
    in_specs = [pl.BlockSpec((tm, D_MODEL), row),
                pl.BlockSpec((1, 1, mod.shape[2]), lambda i: (i // steps_per_mod, 0, 0)),
                _const_spec(g_pre.shape), _const_spec(w_in.shape)]
    args = [x, mod, g_pre, w_in]
    if rope_tabs is not None:
        steps_per_seq = L // tm
        tab = pl.BlockSpec((tm, HEAD_DIM), lambda i: (i % steps_per_seq, 0))
        in_specs += [tab, tab]
        args += list(rope_tabs)
    kv_shape = (tm, KV_WIDTH) if kv_dtype == BF16 else (tm * N_KV_HEADS, HEAD_DIM)
    kv_rows = kv_shape[0] * steps
    out_specs = [pl.BlockSpec((tm, ATTN_WIDTH), row), pl.BlockSpec(kv_shape, row),
                 pl.BlockSpec(kv_shape, row), pl.BlockSpec((KV_WIDTH, tm), lambda i: (0, i)),
                 pl.BlockSpec((tm, 3 * HYENA_WIDTH), row)]
    out_shape = [jax.ShapeDtypeStruct((R, ATTN_WIDTH), BF16),
                 jax.ShapeDtypeStruct((kv_rows, kv_shape[1]), kv_dtype),
                 jax.ShapeDtypeStruct((kv_rows, kv_shape[1]), kv_dtype),
                 jax.ShapeDtypeStruct((KV_WIDTH, R), BF16),
                 jax.ShapeDtypeStruct((R, 3 * HYENA_WIDTH), BF16)]
    if convert_w:
        out_specs.append(pl.BlockSpec(w_in.shape, lambda i: (0, 0)))
        out_shape.append(jax.ShapeDtypeStruct(w_in.shape, BF16))
    cast_specs, cast_shapes = _cast_plan(casts, steps, lambda i: i)
    body = functools.partial(_inproj_kernel, rope=rope_tabs is not None, convert_w=convert_w)
    return pl.pallas_call(
        _with_casts(body, len(args), len(out_specs), len(casts)),
        grid=(steps,),
        in_specs=in_specs + cast_specs,
        out_specs=out_specs + cast_specs,
        out_shape=out_shape + cast_shapes,
        compiler_params=_params("arbitrary"),
        name=f"inproj_{L}",
    )(*args, *casts)


def _rope_tables(L):
    rows = L // GRID_W
    row_ids = np.repeat(np.arange(rows), GRID_W)
    col_ids = np.tile(np.arange(GRID_W), rows)
    half = HEAD_DIM // 2
    inv_freq = ROPE_THETA ** (-np.arange(0, half, 2, dtype=np.float64) / half)
    cos_parts, sin_parts = [], []
    for pos in (row_ids, col_ids):
        ang = pos.astype(np.float64)[:, None] * inv_freq[None, :]
        cos_parts += [np.cos(ang), np.cos(ang)]
        sin_parts += [-np.sin(ang), np.sin(ang)]
    return (jnp.asarray(np.concatenate(cos_parts, axis=-1), F32),
            jnp.asarray(np.concatenate(sin_parts, axis=-1), F32))


def _band_bias():
    key = np.arange(ATTN_KEY_SPAN)[:, None]
    qry = (np.arange(GROUP * ATTN_Q_TILE) % ATTN_Q_TILE)[None, :]
    tabs = [np.where(np.abs(key - off - qry) <= WINDOW, 0.0, NEG_INF) for off in (0, WINDOW, 2 * WINDOW)]
    return jnp.asarray(np.stack(tabs), F32)


def _attn_kernel(*refs, windowed, L, nblk):
    if windowed:
        sink_ref, q_ref, k_ref, vt_ref, ck_ref, cvt_ref, bias_ref, o_ref, s_scr, p_scr, d_scr = refs
    else:
        sink_ref, q_ref, k_ref, vt_ref, o_ref, s_scr, p_scr, d_scr = refs
    tq = ATTN_Q_TILE
    ncol = GROUP * tq
    nkeys = s_scr.shape[1]
    nt = (((1,), (1,)), ((), ()))
    pairs = [(jb, kk) for jb in range(nblk) for kk in range(N_KV_HEADS)]

    def place(c):
        jb, kk = pairs[c]
        rows = slice(jb * tq, (jb + 1) * tq)
        kv = slice(kk * HEAD_DIM, (kk + 1) * HEAD_DIM)
        heads = [kk * GROUP + g for g in range(GROUP)]
        qs = (pl.program_id(1) * nblk + jb) * tq
        start = pl.multiple_of(jnp.clip(qs - WINDOW, 0, L - ATTN_KEY_SPAN), WINDOW) if windowed else 0
        return rows, kv, heads, qs, start

    def scores(c):
        rows, kv, heads, qs, start = place(c)
        q4 = jnp.concatenate([q_ref[rows, hd * HEAD_DIM:(hd + 1) * HEAD_DIM] for hd in heads], axis=0)
        if windowed:
            s_scr[c % 2, :ATTN_KEY_SPAN] = lax.dot_general(
                k_ref[pl.ds(start, ATTN_KEY_SPAN), kv], q4, nt,
                preferred_element_type=F32) + bias_ref[(qs - start) // WINDOW]
            s_scr[c % 2, ATTN_KEY_SPAN:] = lax.dot_general(
                ck_ref[:, kv], q4, nt, preferred_element_type=F32)
        else:
            keys = k_ref[pl.ds(pairs[c][1], L, stride=N_KV_HEADS), :]
            s_scr[c % 2] = lax.dot_general(keys.astype(BF16), q4, nt, preferred_element_type=F32)

    def fold(acc, blk, op):
        for r in range(0, ATTN_KEY_CHUNK, 8):
            acc = op(acc, blk[r:r + 8])
        return acc

    def softmax(c):
        _, _, heads, _, _ = place(c)
        slot = c % 2
        sink = jnp.concatenate([jnp.full((1, tq), sink_ref[hd] * LOG2E, F32) for hd in heads], axis=1)
        m8 = jnp.full((8, ncol), NEG_INF, F32)
        for r in range(0, nkeys, ATTN_KEY_CHUNK):
            m8 = fold(m8, s_scr[slot, r:r + ATTN_KEY_CHUNK, :], jnp.maximum)
        m = jnp.maximum(jnp.max(m8, axis=0, keepdims=True), sink)
        l8 = jnp.zeros((8, ncol), F32)
        for r in range(0, nkeys, ATTN_KEY_CHUNK):
            p = jnp.exp2(s_scr[slot, r:r + ATTN_KEY_CHUNK, :] - m)
            l8 = fold(l8, p, jnp.add)
            p_scr[slot, r:r + ATTN_KEY_CHUNK, :] = p.astype(BF16)
        denom = jnp.sum(l8, axis=0, keepdims=True) + jnp.exp2(sink - m)
        d_scr[slot] = jnp.broadcast_to(1.0 / denom, (8, ncol))

    def values(c):
        rows, kv, heads, _, start = place(c)
        slot = c % 2
        if windowed:
            o = (_dot(vt_ref[kv, pl.ds(start, ATTN_KEY_SPAN)], p_scr[slot, :ATTN_KEY_SPAN])
                 + _dot(cvt_ref[kv, :], p_scr[slot, ATTN_KEY_SPAN:]))
        else:
            o = _dot(vt_ref[kv, :], p_scr[slot])
        o = (o * d_scr[slot, 0:1, :]).T
        for g, hd in enumerate(heads):
            o_ref[rows, hd * HEAD_DIM:(hd + 1) * HEAD_DIM] = o[g * tq:(g + 1) * tq].astype(o_ref.dtype)

    n = len(pairs)
    scores(0)
    scores(1)
    softmax(0)
    for c in range(n):
        if c + 2 < n:
            scores(c + 2)
        if c + 1 < n:
            softmax(c + 1)
        values(c)


def _attention(q, k, vt, sink, B, L, ctx_k=None, ctx_vt=None, casts=()):
    windowed = ctx_k is not None
    nblk = min(L // ATTN_Q_TILE, 4)
    tstep = nblk * ATTN_Q_TILE
    nq = L // tstep
    sink_spec = pl.BlockSpec(memory_space=pltpu.SMEM)
    q_spec = pl.BlockSpec((tstep, ATTN_WIDTH), lambda b, i: (b * nq + i, 0))
    k_block = (L, KV_WIDTH) if windowed else (L * N_KV_HEADS, HEAD_DIM)
    in_specs = [sink_spec, q_spec,
                pl.BlockSpec(k_block, lambda b, i: (b, 0)),
                pl.BlockSpec((KV_WIDTH, L), lambda b, i: (0, b))]
    args = [sink, q, k, vt]
    nkeys, ncol = L, GROUP * ATTN_Q_TILE
    if windowed:
        assert L >= ATTN_KEY_SPAN + WINDOW
        P = ctx_k.shape[0] // B
        nkeys = ATTN_KEY_SPAN + P
        bias = _band_bias()
        in_specs += [pl.BlockSpec((P, KV_WIDTH), lambda b, i: (b, 0)),
                     pl.BlockSpec((KV_WIDTH, P), lambda b, i: (b, 0)),
                     _const_spec(bias.shape)]
        args += [ctx_k, ctx_vt, bias]
    cast_specs, cast_shapes = _cast_plan(casts, B * nq, lambda b, i: b * nq + i)
    body = functools.partial(_attn_kernel, windowed=windowed, L=L, nblk=nblk)
    out = pl.pallas_call(
        _with_casts(body, len(args), 1, len(casts)),
        grid=(B, nq),
        in_specs=in_specs + cast_specs,
        out_specs=[q_spec] + cast_specs,
        out_shape=[jax.ShapeDtypeStruct((B * L, ATTN_WIDTH), BF16)] + cast_shapes,
        scratch_shapes=[pltpu.VMEM((2, nkeys, ncol), F32), pltpu.VMEM((2, nkeys, ncol), BF16),
                        pltpu.VMEM((2, 8, ncol), F32)],
        compiler_params=_params("arbitrary", "arbitrary"),
        name=f"attention_{L}",
    )(*args, *casts)
    return out if casts else out[0]


def _hyena_kernel(x0_ref, x1_ref, v_ref, cw0_ref, cw1_ref, cwv_ref, cb0_ref, cb1_ref, cbv_ref,
                  hlr_ref, hli_ref, hhr_ref, hhi_ref, hm_ref, twc_ref, tws_ref, skip_ref,
                  cos_ref, sin_ref, o_ref, z_scr, y_scr, *, L, nseq):
    rows, tc = o_ref.shape
    half = L // 2
    pos = lax.broadcasted_iota(jnp.int32, (rows, tc), 0) & (L - 1)
    first = pos == 0
    last = pos == L - 1

    def short_conv(u_ref, w_ref, b_ref):
        u = u_ref[...].astype(F32)
        prev = jnp.where(first, 0.0, pltpu.roll(u, 1, axis=0))
        nxt = jnp.where(last, 0.0, pltpu.roll(u, rows - 1, axis=0))
        return prev * w_ref[0:1, :] + u * w_ref[1:2, :] + nxt * w_ref[2:3, :] + b_ref[...]

    def cmul(ar, ai, br, bi):
        return ar * br - ai * bi, ar * bi + ai * br

    lanes = range(tc // LANES)

    def deinterleave(scr, start):
        return jnp.concatenate([scr[j, pl.ds(start, half, stride=2), :] for j in lanes], axis=1)

    z = short_conv(v_ref, cwv_ref, cbv_ref) * short_conv(x1_ref, cw1_ref, cb1_ref)
    for j in lanes:
        z_scr[j] = z[:, j * LANES:(j + 1) * LANES]
    parts = [deinterleave(z_scr, s * L + par) for s in range(nseq) for par in (0, 1)]
    zcat = jnp.concatenate(parts, axis=1).astype(BF16)
    fc = _dot(cos_ref[...], zcat)
    fs = _dot(sin_ref[...], zcat)
    tw_c, tw_s = twc_ref[...], tws_ref[...]
    hlr, hli, hhr, hhi = hlr_ref[...], hli_ref[...], hhr_ref[...], hhi_ref[...]
    sign = jnp.where((lax.broadcasted_iota(jnp.int32, (half, tc), 0) & 1) == 0, 1.0, -1.0)
    g_r, g_i, mids = [], [], []
    for s in range(nseq):
        ev = slice(s * 2 * tc, s * 2 * tc + tc)
        od = slice(s * 2 * tc + tc, (s + 1) * 2 * tc)
        t_r, t_i = cmul(tw_c, -tw_s, fc[:, od], -fs[:, od])
        zl_r, zl_i = fc[:, ev] + t_r, t_i - fs[:, ev]
        zh_r, zh_i = fc[:, ev] - t_r, fs[:, ev] + t_i
        wl_r, wl_i = cmul(zl_r, zl_i, hlr, hli)
        wh_r, wh_i = cmul(zh_r, zh_i, hhr, hhi)
        g1_r, g1_i = cmul(wl_r - wh_r, wl_i + wh_i, tw_c, tw_s)
        g_r += [wl_r + wh_r, g1_r]
        g_i += [wl_i - wh_i, g1_i]
        e_m = jnp.sum(parts[2 * s] * sign, axis=0, keepdims=True)
        o_m = jnp.sum(parts[2 * s + 1] * sign, axis=0, keepdims=True)
        mids.append(cmul(e_m, -o_m, hm_ref[0:1, :], hm_ref[1:2, :]))
    y2 = (_dot(cos_ref[...], jnp.concatenate(g_r, axis=1).astype(BF16))
          - _dot(sin_ref[...], jnp.concatenate(g_i, axis=1).astype(BF16)))
    for s in range(nseq):
        wm_r, wm_i = mids[s]
        y_e = y2[:, s * 2 * tc:s * 2 * tc + tc] + sign * wm_r
        y_o = y2[:, s * 2 * tc + tc:(s + 1) * 2 * tc] - sign * wm_i
        for j in lanes:
            y_scr[j, pl.ds(s * L, half, stride=2), :] = y_e[:, j * LANES:(j + 1) * LANES]
            y_scr[j, pl.ds(s * L + 1, half, stride=2), :] = y_o[:, j * LANES:(j + 1) * LANES]
    y = jnp.concatenate([y_scr[j] for j in lanes], axis=1) + z * skip_ref[...]
    o_ref[...] = (short_conv(x0_ref, cw0_ref, cb0_ref) * y).astype(o_ref.dtype)


def _hyena(hy, conv_w, conv_b, spectrum, twiddles, skip, B, L, casts=()):
    hlr, hli, hhr, hhi, hm, cos_t, sin_t = spectrum
    C = HYENA_WIDTH
    tc = HYENA_CH_TILE
    nj = C // tc
    half = L // 2
    assert L & (L - 1) == 0
    nseq = min(B, max(1, HYENA_ROWS_PER_STEP // L))
    seq = nseq * L

    def part(p, rows):
        return pl.BlockSpec((rows, tc), lambda b, j: (b if rows == seq else 0, p * nj + j))

    spec_c = pl.BlockSpec((half, tc), lambda b, j: (0, j))
    conv_b = conv_b.reshape(1, 3 * C)
    args = [hy, hy, hy, conv_w, conv_w, conv_w, conv_b, conv_b, conv_b,
            hlr, hli, hhr, hhi, hm, *twiddles, skip.reshape(1, C), cos_t, sin_t]
    cast_specs, cast_shapes = _cast_plan(casts, (B // nseq) * nj, lambda b, j: b * nj + j)
    out = pl.pallas_call(
        _with_casts(functools.partial(_hyena_kernel, L=L, nseq=nseq), len(args), 1, len(casts)),
        grid=(B // nseq, nj),
        in_specs=[part(0, seq), part(1, seq), part(2, seq),
                  part(0, 3), part(1, 3), part(2, 3),
                  part(0, 1), part(1, 1), part(2, 1),
                  spec_c, spec_c, spec_c, spec_c, pl.BlockSpec((8, tc), lambda b, j: (0, j)),
                  spec_c, spec_c, pl.BlockSpec((1, tc), lambda b, j: (0, j)),
                  _const_spec((half, half)), _const_spec((half, half))] + cast_specs,
        out_specs=[pl.BlockSpec((seq, tc), lambda b, j: (b, j))] + cast_specs,
        out_shape=[jax.ShapeDtypeStruct((B * L, C), BF16)] + cast_shapes,
        scratch_shapes=[pltpu.VMEM((tc // LANES, seq, LANES), F32)] * 2,
        compiler_params=_params("arbitrary", "arbitrary"),
        name=f"hyena_{L}",
    )(*args, *casts)
    return out if casts else out[0]


def _merge_ffn_kernel(x_ref, attn_ref, hyo_ref, mod_ref, g_pre_ref, g_post_ref, g_fpre_ref, g_fpost_ref,
                      wg_ref, bg_ref, wpa_ref, wph_ref, wo_ref, wup_ref, wdn_ref, o_ref):
    D = D_MODEL
    m = mod_ref[0]
    sh1, sc1, g1, sh2, sc2, g2 = [m[:, j * D:(j + 1) * D] for j in range(6)]
    x = x_ref[...]
    h = _modulated_norm(x, g_pre_ref[...], sh1, sc1).astype(BF16)
    gate_a = jax.nn.sigmoid(_dot(h, wg_ref[:, :D]) + bg_ref[:, :D])
    merged = gate_a * _dot(attn_ref[...], wpa_ref[...])
    gate_h = jax.nn.sigmoid(_dot(h, wg_ref[:, D:]) + bg_ref[:, D:])
    merged = merged + gate_h * _dot(hyo_ref[...], wph_ref[...])
    mix = _dot(merged.astype(BF16), wo_ref[...])
    x = x + g1 * _rmsnorm(mix, g_post_ref[...])
    h2 = _modulated_norm(x, g_fpre_ref[...], sh2, sc2).astype(BF16)
    f = jnp.zeros(x.shape, F32)
    for c0 in range(0, D_FF, FFN_CHUNK):
        gt = _dot(h2, wup_ref[:, c0:c0 + FFN_CHUNK])
        up = _dot(h2, wup_ref[:, D_FF + c0:D_FF + c0 + FFN_CHUNK])
        f = f + _dot((_silu(gt) * up).astype(BF16), wdn_ref[c0:c0 + FFN_CHUNK, :])
    o_ref[...] = x + g2 * _rmsnorm(f, g_fpost_ref[...])


def _merge_ffn(x, attn, hyo, mod, norms, w_gate, b_gate, w_pa, w_ph, w_o, w_up, w_down, L):
    R = x.shape[0]
    tm = ROW_TILE
    steps_per_mod = (R // mod.shape[0]) // tm
    row = lambda i: (i, 0)
    consts = list(norms) + [w_gate, b_gate, w_pa, w_ph, w_o, w_up, w_down]
    return pl.pallas_call(
        _merge_ffn_kernel,
        grid=(R // tm,),
        in_specs=[pl.BlockSpec((tm, D_MODEL), row), pl.BlockSpec((tm, ATTN_WIDTH), row),
                  pl.BlockSpec((tm, HYENA_WIDTH), row),
                  pl.BlockSpec((1, 1, mod.shape[2]), lambda i: (i // steps_per_mod, 0, 0))]
                 + [_const_spec(a.shape) for a in consts],
        out_specs=pl.BlockSpec((tm, D_MODEL), row),
        out_shape=jax.ShapeDtypeStruct((R, D_MODEL), F32),
        compiler_params=_params("arbitrary"),
        name=f"merge_ffn_{L}",
    )(x, attn, hyo, mod, *consts)


def _filter_features(L):
    t = np.concatenate([np.arange(0, L, 2), np.arange(1, L, 2)]).astype(np.float64)
    t_norm = t / L
    bands = np.linspace(1e-4, FILTER_BANDS - 1, FILTER_BANDS)
    ang = (2.0 * math.pi * t / L)[:, None] * bands[None, :]
    feat = np.concatenate([t_norm[:, None], np.cos(ang), np.sin(ang)], axis=-1)
    feat = np.pad(feat, ((0, 0), (0, FILTER_FEAT_PAD - FILTER_FEAT)))
    return jnp.asarray(feat.T, F32), jnp.asarray(t_norm[:, None], F32)


def _layer(x, mod, p, w, B, L, latent, ctx_k, ctx_v):
    convert = w['w_in'].dtype != BF16
    side = (lambda *names: [w[n] for n in names]) if convert else (lambda *names: [])
    wb = dict(w)
    feat, tnorm = _filter_features(L)
    twiddles = _twiddles(L)
    spectrum = _spectrum(L, feat, tnorm, twiddles, p['filt_w1'], p['filt_b1'], p['filt_freq1'],
                         p['filt_w2'], p['filt_b2'], p['filt_freq2'], p['filt_w3'], p['filt_b3'],
                         p['filt_decay'])
    rope_tabs = _rope_tables(L) if latent else None
    q, k, v, vt, hy, *converted = _inproj(x, mod, p['norm_mix_pre'], w['w_in'], L, rope_tabs,
                                          BF16 if latent else F32, side('w_pa', 'w_o', 'w_ph'))
    if convert:
        wb['w_in'], wb['w_pa'], wb['w_o'], wb['w_ph'] = converted
    attn = _attention(q, k, vt, p['attn_sink'], B, L, ctx_k, ctx_v, side('w_up'))
    hyo = _hyena(hy, p['conv_w'], p['conv_b'], spectrum, twiddles, p['hyena_skip'], B, L,
                 side('w_gate', 'w_down'))
    if convert:
        attn, wb['w_up'] = attn
        hyo, wb['w_gate'], wb['w_down'] = hyo
    norms = [p['norm_mix_pre'], p['norm_mix_post'], p['norm_ffn_pre'], p['norm_ffn_post']]
    y = _merge_ffn(x, attn, hyo, mod, norms, wb['w_gate'], p['b_gate'], wb['w_pa'], wb['w_ph'], wb['w_o'],
                   wb['w_up'], wb['w_down'], L)
    return y, k, v, wb


def kernel(x_prompt, x_sample, c, cache_k, cache_v, c_ctx, norm_mix_pre, norm_mix_post, norm_ffn_pre, norm_ffn_post, w_mod, b_mod, w_in, attn_sink, conv_w, conv_b, filt_w1, filt_b1, filt_freq1, filt_w2, filt_b2, filt_freq2, filt_w3, filt_b3, filt_decay, hyena_skip, w_pa, w_ph, w_gate, b_gate, w_o, w_up, w_down):
    Bp, Lp, D = x_prompt.shape
    Bs, Ls, _ = x_sample.shape
    depth = w_in.shape[0]
    past = cache_k.shape[2]
    yp = x_prompt.reshape(Bp * Lp, D)
    ys = x_sample.reshape(Bs * Ls, D)
    cond = jnp.concatenate([c_ctx[None, :], c, jnp.zeros((8 - 1 - Bs, D), F32)], axis=0)
    new_k, new_v = [], []
    row2 = lambda a: a.reshape(1, -1)
    col2 = lambda a: a.reshape(-1, 1)
    for l in range(depth):
        w1 = jnp.pad(filt_w1[l], ((0, FILTER_FEAT_PAD - FILTER_FEAT), (0, 0)))
        p = {
            'norm_mix_pre': row2(norm_mix_pre[l]), 'norm_mix_post': row2(norm_mix_post[l]),
            'norm_ffn_pre': row2(norm_ffn_pre[l]), 'norm_ffn_post': row2(norm_ffn_post[l]),
            'attn_sink': attn_sink[l], 'conv_w': conv_w[l], 'conv_b': conv_b[l],
            'filt_w1': w1.T, 'filt_b1': col2(filt_b1[l]), 'filt_freq1': col2(filt_freq1[l]),
            'filt_w2': filt_w2[l].T, 'filt_b2': col2(filt_b2[l]), 'filt_freq2': col2(filt_freq2[l]),
            'filt_w3': filt_w3[l], 'filt_b3': row2(filt_b3[l]), 'filt_decay': row2(filt_decay[l]),
            'hyena_skip': hyena_skip[l], 'b_gate': row2(b_gate[l]),
        }
        w = {'w_in': w_in[l], 'w_gate': w_gate[l], 'w_pa': w_pa[l], 'w_ph': w_ph[l], 'w_o': w_o[l],
             'w_up': w_up[l], 'w_down': w_down[l]}
        mod = _modulation(cond, w_mod[l], b_mod[l])
        mod_p = mod[0:1].reshape(1, 1, 6 * D)
        mod_s = mod[1:1 + Bs].reshape(Bs, 1, 6 * D)
        yp, k_ctx, v_ctx, w = _layer(yp, mod_p, p, w, Bp, Lp, False, None, None)
        new_k.append(k_ctx.reshape(Bp, Lp, N_KV_HEADS, HEAD_DIM))
        new_v.append(v_ctx.reshape(Bp, Lp, N_KV_HEADS, HEAD_DIM))
        ck = cache_k[:, l].reshape(Bs * past, KV_WIDTH).astype(BF16)
        cv = jnp.swapaxes(cache_v[:, l].reshape(Bs, past, KV_WIDTH), 1, 2).reshape(Bs * KV_WIDTH, past).astype(BF16)
        ys, _, _, _ = _layer(ys, mod_s, p, w, Bs, Ls, True, ck, cv)
    return (yp.reshape(Bp, Lp, D), ys.reshape(Bs, Ls, D),
            jnp.stack(new_k, axis=1), jnp.stack(new_v, axis=1))
```

```python
import functools
import math

import numpy as np
import jax
import jax.numpy as jnp
from jax import lax
from jax.experimental import pallas as pl
from jax.experimental.pallas import tpu as pltpu

D_MODEL = 1024
GRID_W = 64
N_HEADS = 8
N_KV_HEADS = 2
HEAD_DIM = 128
GROUP = N_HEADS // N_KV_HEADS
ATTN_WIDTH = N_HEADS * HEAD_DIM
KV_WIDTH = N_KV_HEADS * HEAD_DIM
WINDOW = 128
ATTN_SCALE = HEAD_DIM ** -0.5
LOG2E = math.log2(math.e)
ROPE_THETA = 10000.0
HYENA_WIDTH = D_MODEL // 2
FILTER_BANDS = 8
FILTER_FEAT = 1 + 2 * FILTER_BANDS
FILTER_FEAT_PAD = 32
FILTER_HIDDEN = 64
D_FF = -(-8 * D_MODEL // (3 * 256)) * 256
IN_WIDTH = ATTN_WIDTH + 2 * KV_WIDTH + 3 * HYENA_WIDTH
EPS = 1e-6
NEG_INF = -1e30

V7X_VMEM_BYTES = 64 * 1024 * 1024
VMEM_LIMIT = V7X_VMEM_BYTES - 8 * 1024 * 1024
MXU_COLS = 256
LANES = 128
FFN_CHUNK = MXU_COLS
ROW_TILE = 512
ATTN_KEY_CHUNK = 64
ATTN_Q_TILE = 128
ATTN_BLOCKS_PER_STEP = 8
ATTN_KEY_SPAN = ATTN_Q_TILE + 2 * WINDOW
DFT_FINE_ROWS = 32
HYENA_CH_TILE = 256
HYENA_ROWS_PER_STEP = 2048

BF16 = jnp.bfloat16
F32 = jnp.float32


def _params(*semantics):
    return pltpu.CompilerParams(dimension_semantics=semantics, vmem_limit_bytes=VMEM_LIMIT)


def _const_spec(shape):
    zeros = (0,) * len(shape)
    return pl.BlockSpec(shape, lambda *_: zeros, pipeline_mode=pl.Buffered(1))


def _dot(a, b):
    return jnp.dot(a, b, preferred_element_type=F32)


def _with_casts(body, n_in, n_out, ncast):
    if ncast == 0:
        return body

    def kernel(*refs):
        ins, rest = refs[:n_in], refs[n_in:]
        cast_in, rest = rest[:ncast], rest[ncast:]
        outs, rest = rest[:n_out], rest[n_out:]
        cast_out, scratch = rest[:ncast], rest[ncast:]
        for src, dst in zip(cast_in, cast_out):
            dst[...] = src[...].astype(dst.dtype)
        body(*ins, *outs, *scratch)

    return kernel


def _cast_plan(casts, nsteps, linear_step):
    specs, shapes = [], []
    for a in casts:
        rows, cols = a.shape
        slab = rows // nsteps
        assert slab * nsteps == rows and slab % 16 == 0, (a.shape, nsteps)
        specs.append(pl.BlockSpec((slab, cols), lambda *g: (linear_step(*g), 0)))
        shapes.append(jax.ShapeDtypeStruct(a.shape, BF16))
    return specs, shapes


def _rmsnorm(x, g):
    return x * lax.rsqrt(jnp.mean(x * x, axis=-1, keepdims=True) + EPS) * g


def _silu(x):
    return x * jax.nn.sigmoid(x)


def _mod_kernel(cond_ref, w_ref, b_ref, o_ref):
    a = _silu(cond_ref[...]).astype(BF16)
    o_ref[...] = _dot(a, w_ref[...].astype(BF16)) + b_ref[...]


def _modulation(cond, w_mod, b_mod):
    rows = cond.shape[0]
    n = w_mod.shape[1]
    tn = D_MODEL
    return pl.pallas_call(
        _mod_kernel,
        grid=(n // tn,),
        in_specs=[pl.BlockSpec((rows, D_MODEL), lambda j: (0, 0)),
                  pl.BlockSpec((D_MODEL, tn), lambda j: (0, j)),
                  pl.BlockSpec((1, tn), lambda j: (0, j))],
        out_specs=pl.BlockSpec((rows, tn), lambda j: (0, j)),
        out_shape=jax.ShapeDtypeStruct((rows, n), F32),
        compiler_params=_params("arbitrary"),
        name="modulation",
    )(cond, w_mod, b_mod.reshape(1, n))


def _dft_seed_tables(n):
    bs = DFT_FINE_ROWS
    t = np.arange(n, dtype=np.int64)
    theta = math.pi / n
    ang_a = ((np.arange(n // bs, dtype=np.int64)[:, None] * bs * t[None, :]) % (2 * n)) * theta
    ang_b = ((np.arange(bs, dtype=np.int64)[:, None] * t[None, :]) % (2 * n)) * theta
    return [jnp.asarray(f(a), F32) for a in (ang_a, ang_b) for f in (np.cos, np.sin)]


def _twiddles(L):
    ang = (math.pi / L) * np.arange(L // 2, dtype=np.float64)[:, None] * np.ones((1, HYENA_WIDTH))
    return jnp.asarray(np.cos(ang), F32), jnp.asarray(np.sin(ang), F32)


def _spectrum_kernel(feat_ref, w1_ref, b1_ref, f1_ref, w2_ref, b2_ref, f2_ref, w3_ref, b3_ref,
                     decay_ref, tn_ref, ca_ref, sa_ref, cb_ref, sb_ref, twc_ref, tws_ref,
                     hlr_ref, hli_ref, hhr_ref, hhi_ref, hm_ref, cos_ref, sin_ref, opc_s, ops_s, *, L, tk):
    i = pl.program_id(0)
    hi_prec = lax.Precision.HIGHEST
    C = HYENA_WIDTH
    half = L // 2

    bs = DFT_FINE_ROWS
    cb, sb = cb_ref[...], sb_ref[...]
    for j in range(tk // bs):
        ca = ca_ref[pl.ds(i * (tk // bs) + j, 1), :]
        sa = sa_ref[pl.ds(i * (tk // bs) + j, 1), :]
        cos_ref[j * bs:(j + 1) * bs, :] = (ca * cb - sa * sb).astype(BF16)
        sin_ref[j * bs:(j + 1) * bs, :] = (sa * cb + ca * sb).astype(BF16)

    @pl.when(i == 0)
    def _():
        h = jnp.dot(w1_ref[...], feat_ref[...], precision=hi_prec, preferred_element_type=F32)
        h = jnp.sin(f1_ref[...] * (h + b1_ref[...]))
        h = jnp.dot(w2_ref[...], h, precision=hi_prec, preferred_element_type=F32)
        h = jnp.sin(f2_ref[...] * (h + b2_ref[...]))
        h = _dot(h.T.astype(BF16), w3_ref[...].astype(BF16)) + b3_ref[...]
        h = h * jnp.exp(-tn_ref[...] * jnp.abs(decay_ref[...]))
        fwd = h[:, :C]
        bwd = h[:, C:]
        row = lax.broadcasted_iota(jnp.int32, (L, C), 0)
        bwd = jnp.where(row == 0, 0.0, bwd)
        a = fwd + bwd
        b = fwd - bwd
        a_e, a_o, b_e, b_o = a[:half], a[half:], b[:half], b[half:]
        opc_s[...] = jnp.concatenate([a_e, a_o, b_o], axis=1).astype(BF16)
        ops_s[...] = jnp.concatenate([a_o, b_e, b_o], axis=1).astype(BF16)
        sign = jnp.where((lax.broadcasted_iota(jnp.int32, (half, C), 0) & 1) == 0, 1.0, -1.0)
        mid_r = jnp.sum(a_e * sign, axis=0, keepdims=True)
        mid_i = -jnp.sum(b_o * sign, axis=0, keepdims=True)
        hm_ref[...] = jnp.concatenate([mid_r, mid_i, jnp.zeros((6, C), F32)], axis=0) * (1.0 / L)

    pc = _dot(cos_ref[...], opc_s[...])
    ps = _dot(sin_ref[...], ops_s[...])
    ae_c, ao_c, bo_c = pc[:, :C], pc[:, C:2 * C], pc[:, 2 * C:]
    ao_s, be_s, bo_s = ps[:, :C], ps[:, C:2 * C], ps[:, 2 * C:]
    tw_c, tw_s = twc_ref[...], tws_ref[...]
    t_r = tw_c * ao_c - tw_s * ao_s
    t_i = tw_c * bo_s + tw_s * bo_c
    krow = lax.broadcasted_iota(jnp.int32, (tk, C), 0) + i * tk
    wk = jnp.where(krow == 0, 1.0, 2.0) * (1.0 / (2 * L))
    hlr_ref[...] = (ae_c + t_r) * wk
    hhr_ref[...] = (ae_c - t_r) * wk
    hli_ref[...] = (be_s + t_i) * (-wk)
    hhi_ref[...] = (be_s - t_i) * wk


def _spectrum(L, feat, tnorm, twiddles, w1, b1, f1, w2, b2, f2, w3, b3, decay):
    half = L // 2
    tk = min(half, 512)
    C = HYENA_WIDTH
    small = [feat, w1, b1, f1, w2, b2, f2, w3, b3, decay, tnorm] + _dft_seed_tables(half)
    blk = lambda w: pl.BlockSpec((tk, w), lambda i: (i, 0))
    return pl.pallas_call(
        functools.partial(_spectrum_kernel, L=L, tk=tk),
        grid=(half // tk,),
        in_specs=[_const_spec(a.shape) for a in small] + [blk(C), blk(C)],
        out_specs=[blk(C), blk(C), blk(C), blk(C), pl.BlockSpec((8, C), lambda i: (0, 0)), blk(half), blk(half)],
        out_shape=[jax.ShapeDtypeStruct((half, C), F32)] * 4
                  + [jax.ShapeDtypeStruct((8, C), F32),
                     jax.ShapeDtypeStruct((half, half), BF16), jax.ShapeDtypeStruct((half, half), BF16)],
        scratch_shapes=[pltpu.VMEM((half, 3 * C), BF16), pltpu.VMEM((half, 3 * C), BF16)],
        compiler_params=_params("arbitrary"),
        name=f"spectrum_{L}",
    )(*small, *twiddles)


def _modulated_norm(x, g, shift, scale):
    return _rmsnorm(x, g) * (1.0 + scale) + shift


def _rope(x, cos, sin_signed):
    lane = lax.broadcasted_iota(jnp.int32, x.shape, 1)
    swapped = jnp.where((lane & 63) < 32,
                        pltpu.roll(x, HEAD_DIM - 32, axis=1),
                        pltpu.roll(x, 32, axis=1))
    return x * cos + swapped * sin_signed


def _inproj_kernel(*refs, rope, convert_w):
    refs = list(refs)
    wb_ref = refs.pop() if convert_w else None
    if rope:
        x_ref, mod_ref, g_ref, w_ref, cos_ref, sin_ref, q_ref, k_ref, v_ref, vt_ref, hy_ref = refs
    else:
        x_ref, mod_ref, g_ref, w_ref, q_ref, k_ref, v_ref, vt_ref, hy_ref = refs
    if convert_w:
        @pl.when(pl.program_id(0) == 0)
        def _():
            wb_ref[...] = w_ref[...].astype(BF16)
        w_ref = wb_ref
    m = mod_ref[0]
    h = _modulated_norm(x_ref[...], g_ref[...], m[:, 0:D_MODEL], m[:, D_MODEL:2 * D_MODEL]).astype(BF16)

    def heads(y, out_ref, n, scale):
        if not rope:
            out_ref[...] = (y if scale == 1.0 else y * scale).astype(out_ref.dtype)
            return
        cos, sin = cos_ref[...], sin_ref[...]
        if scale != 1.0:
            cos, sin = cos * scale, sin * scale
        for hd in range(n):
            cols = slice(hd * HEAD_DIM, (hd + 1) * HEAD_DIM)
            out_ref[:, cols] = _rope(y[:, cols], cos, sin).astype(out_ref.dtype)

    heads(_dot(h, w_ref[:, :ATTN_WIDTH]), q_ref, N_HEADS, ATTN_SCALE * LOG2E)
    kv = _dot(h, w_ref[:, ATTN_WIDTH:ATTN_WIDTH + 2 * KV_WIDTH])
    def store_kv(y, out_ref):
        if out_ref.shape[1] == KV_WIDTH:
            out_ref[...] = y.astype(out_ref.dtype)
            return
        for hd in range(N_KV_HEADS):
            out_ref[pl.ds(hd, y.shape[0], stride=N_KV_HEADS), :] = (
                y[:, hd * HEAD_DIM:(hd + 1) * HEAD_DIM].astype(out_ref.dtype))

    if rope:
        heads(kv[:, :KV_WIDTH], k_ref, N_KV_HEADS, 1.0)
    else:
        store_kv(kv[:, :KV_WIDTH], k_ref)
    v = kv[:, KV_WIDTH:]
    store_kv(v, v_ref)
    vt_ref[...] = v.T.astype(vt_ref.dtype)
    hy_ref[...] = _dot(h, w_ref[:, ATTN_WIDTH + 2 * KV_WIDTH:]).astype(hy_ref.dtype)


def _inproj(x, mod, g_pre, w_in, L, rope_tabs, kv_dtype, casts=()):
    R = x.shape[0]
    tm = ROW_TILE
    steps = R // tm
    steps_per_mod = (R // mod.shape[0]) // tm
    convert_w = w_in.dtype != BF16
    row = lambda i: (i, 0)
    in_specs = [pl.BlockSpec((tm, D_MODEL), row),
                pl.BlockSpec((1, 1, mod.shape[2]), lambda i: (i // steps_per_mod, 0, 0)),
                _const_spec(g_pre.shape), _const_spec(w_in.shape)]
    args = [x, mod, g_pre, w_in]
    if rope_tabs is not None:
        steps_per_seq = L // tm
        tab = pl.BlockSpec((tm, HEAD_DIM), lambda i: (i % steps_per_seq, 0))
        in_specs += [tab, tab]
        args += list(rope_tabs)
    kv_shape = (tm, KV_WIDTH) if kv_dtype == BF16 else (tm * N_KV_HEADS, HEAD_DIM)
    kv_rows = kv_shape[0] * steps
    out_specs = [pl.BlockSpec((tm, ATTN_WIDTH), row), pl.BlockSpec(kv_shape, row),
                 pl.BlockSpec(kv_shape, row), pl.BlockSpec((KV_WIDTH, tm), lambda i: (0, i)),
                 pl.BlockSpec((tm, 3 * HYENA_WIDTH), row)]
    out_shape = [jax.ShapeDtypeStruct((R, ATTN_WIDTH), BF16),
                 jax.ShapeDtypeStruct((kv_rows, kv_shape[1]), kv_dtype),
                 jax.ShapeDtypeStruct((kv_rows, kv_shape[1]), kv_dtype),
                 jax.ShapeDtypeStruct((KV_WIDTH, R), BF16),
                 jax.ShapeDtypeStruct((R, 3 * HYENA_WIDTH), BF16)]
    if convert_w:
        out_specs.append(pl.BlockSpec(w_in.shape, lambda i: (0, 0)))
        out_shape.append(jax.ShapeDtypeStruct(w_in.shape, BF16))
    cast_specs, cast_shapes = _cast_plan(casts, steps, lambda i: i)
    body = functools.partial(_inproj_kernel, rope=rope_tabs is not None, convert_w=convert_w)
    return pl.pallas_call(
        _with_casts(body, len(args), len(out_specs), len(casts)),
        grid=(steps,),
        in_specs=in_specs + cast_specs,
        out_specs=out_specs + cast_specs,
        out_shape=out_shape + cast_shapes,
        compiler_params=_params("arbitrary"),
        name=f"inproj_{L}",
    )(*args, *casts)


def _rope_tables(L):
    rows = L // GRID_W
    row_ids = np.repeat(np.arange(rows), GRID_W)
    col_ids = np.tile(np.arange(GRID_W), rows)
    half = HEAD_DIM // 2
    inv_freq = ROPE_THETA ** (-np.arange(0, half, 2, dtype=np.float64) / half)
    cos_parts, sin_parts = [], []
    for pos in (row_ids, col_ids):
        ang = pos.astype(np.float64)[:, None] * inv_freq[None, :]
        cos_parts += [np.cos(ang), np.cos(ang)]
        sin_parts += [-np.sin(ang), np.sin(ang)]
    return (jnp.asarray(np.concatenate(cos_parts, axis=-1), F32),
            jnp.asarray(np.concatenate(sin_parts, axis=-1), F32))


def _band_bias():
    key = np.arange(ATTN_KEY_SPAN)[:, None]
    qry = (np.arange(GROUP * ATTN_Q_TILE) % ATTN_Q_TILE)[None, :]
    tabs = [np.where(np.abs(key - off - qry) <= WINDOW, 0.0, NEG_INF) for off in (0, WINDOW, 2 * WINDOW)]
    return jnp.asarray(np.stack(tabs), F32)


def _attn_kernel(*refs, windowed, L, nblk):
    if windowed:
        sink_ref, q_ref, k_ref, vt_ref, ck_ref, cvt_ref, bias_ref, o_ref, s_scr, p_scr, d_scr = refs
    else:
        sink_ref, q_ref, k_ref, vt_ref, o_ref, s_scr, p_scr, d_scr = refs
    tq = ATTN_Q_TILE
    ncol = GROUP * tq
    nkeys = s_scr.shape[1]
    nt = (((1,), (1,)), ((), ()))
    pairs = [(jb, kk) for jb in range(nblk) for kk in range(N_KV_HEADS)]

    def place(c):
        jb, kk = pairs[c]
        rows = slice(jb * tq, (jb + 1) * tq)
        kv = slice(kk * HEAD_DIM, (kk + 1) * HEAD_DIM)
        heads = [kk * GROUP + g for g in range(GROUP)]
        qs = (pl.program_id(1) * nblk + jb) * tq
        start = pl.multiple_of(jnp.clip(qs - WINDOW, 0, L - ATTN_KEY_SPAN), WINDOW) if windowed else 0
        return rows, kv, heads, qs, start

    def scores(c):
        rows, kv, heads, qs, start = place(c)
        q4 = jnp.concatenate([q_ref[rows, hd * HEAD_DIM:(hd + 1) * HEAD_DIM] for hd in heads], axis=0)
        if windowed:
            s_scr[c % 2, :ATTN_KEY_SPAN] = lax.dot_general(
                k_ref[pl.ds(start, ATTN_KEY_SPAN), kv], q4, nt,
                preferred_element_type=F32) + bias_ref[(qs - start) // WINDOW]
            s_scr[c % 2, ATTN_KEY_SPAN:] = lax.dot_general(
                ck_ref[:, kv], q4, nt, preferred_element_type=F32)
        else:
            keys = k_ref[pl.ds(pairs[c][1], L, stride=N_KV_HEADS), :]
            s_scr[c % 2] = lax.dot_general(keys.astype(BF16), q4, nt, preferred_element_type=F32)

    def fold(acc, blk, op):
        for r in range(0, ATTN_KEY_CHUNK, 8):
            acc = op(acc, blk[r:r + 8])
        return acc

    def softmax(c):
        _, _, heads, _, _ = place(c)
        slot = c % 2
        sink = jnp.concatenate([jnp.full((1, tq), sink_ref[hd] * LOG2E, F32) for hd in heads], axis=1)
        m8 = jnp.full((8, ncol), NEG_INF, F32)
        for r in range(0, nkeys, ATTN_KEY_CHUNK):
            m8 = fold(m8, s_scr[slot, r:r + ATTN_KEY_CHUNK, :], jnp.maximum)
        m = jnp.maximum(jnp.max(m8, axis=0, keepdims=True), sink)
        l8 = jnp.zeros((8, ncol), F32)
        for r in range(0, nkeys, ATTN_KEY_CHUNK):
            p = jnp.exp2(s_scr[slot, r:r + ATTN_KEY_CHUNK, :] - m)
            l8 = fold(l8, p, jnp.add)
            p_scr[slot, r:r + ATTN_KEY_CHUNK, :] = p.astype(BF16)
        denom = jnp.sum(l8, axis=0, keepdims=True) + jnp.exp2(sink - m)
        d_scr[slot] = jnp.broadcast_to(1.0 / denom, (8, ncol))

    def values(c):
        rows, kv, heads, _, start = place(c)
        slot = c % 2
        if windowed:
            o = (_dot(vt_ref[kv, pl.ds(start, ATTN_KEY_SPAN)], p_scr[slot, :ATTN_KEY_SPAN])
                 + _dot(cvt_ref[kv, :], p_scr[slot, ATTN_KEY_SPAN:]))
        else:
            o = _dot(vt_ref[kv, :], p_scr[slot])
        o = (o * d_scr[slot, 0:1, :]).T
        for g, hd in enumerate(heads):
            o_ref[rows, hd * HEAD_DIM:(hd + 1) * HEAD_DIM] = o[g * tq:(g + 1) * tq].astype(o_ref.dtype)

    n = len(pairs)
    scores(0)
    scores(1)
    softmax(0)
    for c in range(n):
        if c + 2 < n:
            scores(c + 2)
        if c + 1 < n:
            softmax(c + 1)
        values(c)


def _attention(q, k, vt, sink, B, L, ctx_k=None, ctx_vt=None, casts=()):
    windowed = ctx_k is not None
    nblk = min(L // ATTN_Q_TILE, ATTN_BLOCKS_PER_STEP)
    tstep = nblk * ATTN_Q_TILE
    nq = L // tstep
    sink_spec = pl.BlockSpec(memory_space=pltpu.SMEM)
    q_spec = pl.BlockSpec((tstep, ATTN_WIDTH), lambda b, i: (b * nq + i, 0))
    k_block = (L, KV_WIDTH) if windowed else (L * N_KV_HEADS, HEAD_DIM)
    in_specs = [sink_spec, q_spec,
                pl.BlockSpec(k_block, lambda b, i: (b, 0)),
                pl.BlockSpec((KV_WIDTH, L), lambda b, i: (0, b))]
    args = [sink, q, k, vt]
    nkeys, ncol = L, GROUP * ATTN_Q_TILE
    if windowed:
        assert L >= ATTN_KEY_SPAN + WINDOW
        P = ctx_k.shape[0] // B
        nkeys = ATTN_KEY_SPAN + P
        bias = _band_bias()
        in_specs += [pl.BlockSpec((P, KV_WIDTH), lambda b, i: (b, 0)),
                     pl.BlockSpec((KV_WIDTH, P), lambda b, i: (b, 0)),
                     _const_spec(bias.shape)]
        args += [ctx_k, ctx_vt, bias]
    cast_specs, cast_shapes = _cast_plan(casts, B * nq, lambda b, i: b * nq + i)
    body = functools.partial(_attn_kernel, windowed=windowed, L=L, nblk=nblk)
    out = pl.pallas_call(
        _with_casts(body, len(args), 1, len(casts)),
        grid=(B, nq),
        in_specs=in_specs + cast_specs,
        out_specs=[q_spec] + cast_specs,
        out_shape=[jax.ShapeDtypeStruct((B * L, ATTN_WIDTH), BF16)] + cast_shapes,
        scratch_shapes=[pltpu.VMEM((2, nkeys, ncol), F32), pltpu.VMEM((2, nkeys, ncol), BF16),
                        pltpu.VMEM((2, 8, ncol), F32)],
        compiler_params=_params("arbitrary", "arbitrary"),
        name=f"attention_{L}",
    )(*args, *casts)
    return out if casts else out[0]


def _hyena_kernel(x0_ref, x1_ref, v_ref, cw0_ref, cw1_ref, cwv_ref, cb0_ref, cb1_ref, cbv_ref,
                  hlr_ref, hli_ref, hhr_ref, hhi_ref, hm_ref, twc_ref, tws_ref, skip_ref,
                  cos_ref, sin_ref, o_ref, z_scr, y_scr, *, L, nseq):
    rows, tc = o_ref.shape
    half = L // 2
    pos = lax.broadcasted_iota(jnp.int32, (rows, tc), 0) & (L - 1)
    first = pos == 0
    last = pos == L - 1

    def short_conv(u_ref, w_ref, b_ref):
        u = u_ref[...].astype(F32)
        prev = jnp.where(first, 0.0, pltpu.roll(u, 1, axis=0))
        nxt = jnp.where(last, 0.0, pltpu.roll(u, rows - 1, axis=0))
        return prev * w_ref[0:1, :] + u * w_ref[1:2, :] + nxt * w_ref[2:3, :] + b_ref[...]

    def cmul(ar, ai, br, bi):
        return ar * br - ai * bi, ar * bi + ai * br

    lanes = range(tc // LANES)

    def deinterleave(scr, start):
        return jnp.concatenate([scr[j, pl.ds(start, half, stride=2), :] for j in lanes], axis=1)

    z = short_conv(v_ref, cwv_ref, cbv_ref) * short_conv(x1_ref, cw1_ref, cb1_ref)
    for j in lanes:
        z_scr[j] = z[:, j * LANES:(j + 1) * LANES]
    parts = [deinterleave(z_scr, s * L + par) for s in range(nseq) for par in (0, 1)]
    zcat = jnp.concatenate(parts, axis=1).astype(BF16)
    fc = _dot(cos_ref[...], zcat)
    fs = _dot(sin_ref[...], zcat)
    tw_c, tw_s = twc_ref[...], tws_ref[...]
    hlr, hli, hhr, hhi = hlr_ref[...], hli_ref[...], hhr_ref[...], hhi_ref[...]
    sign = jnp.where((lax.broadcasted_iota(jnp.int32, (half, tc), 0) & 1) == 0, 1.0, -1.0)
    g_r, g_i, mids = [], [], []
    for s in range(nseq):
        ev = slice(s * 2 * tc, s * 2 * tc + tc)
        od = slice(s * 2 * tc + tc, (s + 1) * 2 * tc)
        t_r, t_i = cmul(tw_c, -tw_s, fc[:, od], -fs[:, od])
        zl_r, zl_i = fc[:, ev] + t_r, t_i - fs[:, ev]
        zh_r, zh_i = fc[:, ev] - t_r, fs[:, ev] + t_i
        wl_r, wl_i = cmul(zl_r, zl_i, hlr, hli)
        wh_r, wh_i = cmul(zh_r, zh_i, hhr, hhi)
        g1_r, g1_i = cmul(wl_r - wh_r, wl_i + wh_i, tw_c, tw_s)
        g_r += [wl_r + wh_r, g1_r]
        g_i += [wl_i - wh_i, g1_i]
        e_m = jnp.sum(parts[2 * s] * sign, axis=0, keepdims=True)
        o_m = jnp.sum(parts[2 * s + 1] * sign, axis=0, keepdims=True)
        mids.append(cmul(e_m, -o_m, hm_ref[0:1, :], hm_ref[1:2, :]))
    y2 = (_dot(cos_ref[...], jnp.concatenate(g_r, axis=1).astype(BF16))
          - _dot(sin_ref[...], jnp.concatenate(g_i, axis=1).astype(BF16)))
    for s in range(nseq):
        wm_r, wm_i = mids[s]
        y_e = y2[:, s * 2 * tc:s * 2 * tc + tc] + sign * wm_r
        y_o = y2[:, s * 2 * tc + tc:(s + 1) * 2 * tc] - sign * wm_i
        for j in lanes:
            y_scr[j, pl.ds(s * L, half, stride=2), :] = y_e[:, j * LANES:(j + 1) * LANES]
            y_scr[j, pl.ds(s * L + 1, half, stride=2), :] = y_o[:, j * LANES:(j + 1) * LANES]
    y = jnp.concatenate([y_scr[j] for j in lanes], axis=1) + z * skip_ref[...]
    o_ref[...] = (short_conv(x0_ref, cw0_ref, cb0_ref) * y).astype(o_ref.dtype)


def _hyena(hy, conv_w, conv_b, spectrum, twiddles, skip, B, L, casts=()):
    hlr, hli, hhr, hhi, hm, cos_t, sin_t = spectrum
    C = HYENA_WIDTH
    tc = HYENA_CH_TILE
    nj = C // tc
    half = L // 2
    assert L & (L - 1) == 0
    nseq = min(B, max(1, HYENA_ROWS_PER_STEP // L))
    seq = nseq * L

    def part(p, rows):
        return pl.BlockSpec((rows, tc), lambda b, j: (b if rows == seq else 0, p * nj + j))

    spec_c = pl.BlockSpec((half, tc), lambda b, j: (0, j))
    conv_b = conv_b.reshape(1, 3 * C)
    args = [hy, hy, hy, conv_w, conv_w, conv_w, conv_b, conv_b, conv_b,
            hlr, hli, hhr, hhi, hm, *twiddles, skip.reshape(1, C), cos_t, sin_t]
    cast_specs, cast_shapes = _cast_plan(casts, (B // nseq) * nj, lambda b, j: b * nj + j)
    out = pl.pallas_call(
        _with_casts(functools.partial(_hyena_kernel, L=L, nseq=nseq), len(args), 1, len(casts)),
        grid=(B // nseq, nj),
        in_specs=[part(0, seq), part(1, seq), part(2, seq),
                  part(0, 3), part(1, 3), part(2, 3),
                  part(0, 1), part(1, 1), part(2, 1),
                  spec_c, spec_c, spec_c, spec_c, pl.BlockSpec((8, tc), lambda b, j: (0, j)),
                  spec_c, spec_c, pl.BlockSpec((1, tc), lambda b, j: (0, j)),
                  _const_spec((half, half)), _const_spec((half, half))] + cast_specs,
        out_specs=[pl.BlockSpec((seq, tc), lambda b, j: (b, j))] + cast_specs,
        out_shape=[jax.ShapeDtypeStruct((B * L, C), BF16)] + cast_shapes,
        scratch_shapes=[pltpu.VMEM((tc // LANES, seq, LANES), F32)] * 2,
        compiler_params=_params("arbitrary", "arbitrary"),
        name=f"hyena_{L}",
    )(*args, *casts)
    return out if casts else out[0]


def _merge_ffn_rows(x_ref, attn_ref, hyo_ref, mod_ref, g_pre_ref, g_post_ref, g_fpre_ref, g_fpost_ref,
                    wg_ref, bg_ref, wpa_ref, wph_ref, wo_ref, wup_ref, wdn_ref, o_ref):
    D = D_MODEL
    m = mod_ref[0]
    sh1, sc1, g1, sh2, sc2, g2 = [m[:, j * D:(j + 1) * D] for j in range(6)]
    x = x_ref[...]
    h = _modulated_norm(x, g_pre_ref[...], sh1, sc1).astype(BF16)
    gate_a = jax.nn.sigmoid(_dot(h, wg_ref[:, :D]) + bg_ref[:, :D])
    merged = gate_a * _dot(attn_ref[...], wpa_ref[...])
    gate_h = jax.nn.sigmoid(_dot(h, wg_ref[:, D:]) + bg_ref[:, D:])
    merged = merged + gate_h * _dot(hyo_ref[...], wph_ref[...])
    mix = _dot(merged.astype(BF16), wo_ref[...])
    x = x + g1 * _rmsnorm(mix, g_post_ref[...])
    h2 = _modulated_norm(x, g_fpre_ref[...], sh2, sc2).astype(BF16)
    f = jnp.zeros(x.shape, F32)
    for c0 in range(0, D_FF, FFN_CHUNK):
        gt = _dot(h2, wup_ref[:, c0:c0 + FFN_CHUNK])
        up = _dot(h2, wup_ref[:, D_FF + c0:D_FF + c0 + FFN_CHUNK])
        f = f + _dot((_silu(gt) * up).astype(BF16), wdn_ref[c0:c0 + FFN_CHUNK, :])
    o_ref[...] = x + g2 * _rmsnorm(f, g_fpost_ref[...])


N_MERGE_CONSTS = 11


def _merge_ffn_kernel(*refs, seg_steps):
    nseg = len(seg_steps)
    ins, mod_ref = refs[:3 * nseg], refs[3 * nseg]
    consts = refs[3 * nseg + 1:3 * nseg + 1 + N_MERGE_CONSTS]
    outs = refs[3 * nseg + 1 + N_MERGE_CONSTS:]
    i = pl.program_id(0)
    off = 0
    for s, n in enumerate(seg_steps):
        pl.when((i >= off) & (i < off + n))(
            functools.partial(_merge_ffn_rows, *ins[3 * s:3 * s + 3], mod_ref, *consts, outs[s]))
        off += n


def _merge_ffn(groups, mod, norms, w_gate, b_gate, w_pa, w_ph, w_o, w_up, w_down):
    tm = ROW_TILE
    consts = list(norms) + [w_gate, b_gate, w_pa, w_ph, w_o, w_up, w_down]
    assert len(consts) == N_MERGE_CONSTS
    seg_steps = [g[0].shape[0] // tm for g in groups]
    offs = [sum(seg_steps[:s]) for s in range(len(groups))]

    def parked(off, n):
        return lambda i: (jnp.clip(i - off, 0, n - 1), 0)

    def mod_row(i):
        idx = 0
        for (x, _, _, first, rows_per_mod), off in zip(groups, offs):
            idx = jnp.where(i >= off, first + ((i - off) * tm) // rows_per_mod, idx)
        return (idx, 0, 0)

    in_specs, args, out_specs, out_shape = [], [], [], []
    for (x, attn, hyo, _, _), off, n in zip(groups, offs, seg_steps):
        in_specs += [pl.BlockSpec((tm, D_MODEL), parked(off, n)), pl.BlockSpec((tm, ATTN_WIDTH), parked(off, n)),
                     pl.BlockSpec((tm, HYENA_WIDTH), parked(off, n))]
        args += [x, attn, hyo]
        out_specs.append(pl.BlockSpec((tm, D_MODEL), parked(off, n)))
        out_shape.append(jax.ShapeDtypeStruct(x.shape, F32))
    return pl.pallas_call(
        functools.partial(_merge_ffn_kernel, seg_steps=seg_steps),
        grid=(sum(seg_steps),),
        in_specs=in_specs + [pl.BlockSpec((1, 1, mod.shape[2]), mod_row)] + [_const_spec(a.shape) for a in consts],
        out_specs=out_specs,
        out_shape=out_shape,
        compiler_params=_params("arbitrary"),
        name="merge_ffn",
    )(*args, mod, *consts)


def _filter_features(L):
    t = np.concatenate([np.arange(0, L, 2), np.arange(1, L, 2)]).astype(np.float64)
    t_norm = t / L
    bands = np.linspace(1e-4, FILTER_BANDS - 1, FILTER_BANDS)
    ang = (2.0 * math.pi * t / L)[:, None] * bands[None, :]
    feat = np.concatenate([t_norm[:, None], np.cos(ang), np.sin(ang)], axis=-1)
    feat = np.pad(feat, ((0, 0), (0, FILTER_FEAT_PAD - FILTER_FEAT)))
    return jnp.asarray(feat.T, F32), jnp.asarray(t_norm[:, None], F32)


def _mixers(x, mod, p, w, B, L, latent, ctx_k, ctx_v):
    convert = w['w_in'].dtype != BF16
    side = (lambda *names: [w[n] for n in names]) if convert else (lambda *names: [])
    wb = dict(w)
    feat, tnorm = _filter_features(L)
    twiddles = _twiddles(L)
    spectrum = _spectrum(L, feat, tnorm, twiddles, p['filt_w1'], p['filt_b1'], p['filt_freq1'],
                         p['filt_w2'], p['filt_b2'], p['filt_freq2'], p['filt_w3'], p['filt_b3'],
                         p['filt_decay'])
    rope_tabs = _rope_tables(L) if latent else None
    q, k, v, vt, hy, *converted = _inproj(x, mod, p['norm_mix_pre'], w['w_in'], L, rope_tabs,
                                          BF16 if latent else F32, side('w_pa', 'w_o', 'w_ph'))
    if convert:
        wb['w_in'], wb['w_pa'], wb['w_o'], wb['w_ph'] = converted
    attn = _attention(q, k, vt, p['attn_sink'], B, L, ctx_k, ctx_v, side('w_up'))
    hyo = _hyena(hy, p['conv_w'], p['conv_b'], spectrum, twiddles, p['hyena_skip'], B, L,
                 side('w_gate', 'w_down'))
    if convert:
        attn, wb['w_up'] = attn
        hyo, wb['w_gate'], wb['w_down'] = hyo
    return attn, hyo, k, v, wb


def kernel(x_prompt, x_sample, c, cache_k, cache_v, c_ctx, norm_mix_pre, norm_mix_post, norm_ffn_pre, norm_ffn_post, w_mod, b_mod, w_in, attn_sink, conv_w, conv_b, filt_w1, filt_b1, filt_freq1, filt_w2, filt_b2, filt_freq2, filt_w3, filt_b3, filt_decay, hyena_skip, w_pa, w_ph, w_gate, b_gate, w_o, w_up, w_down):
    Bp, Lp, D = x_prompt.shape
    Bs, Ls, _ = x_sample.shape
    depth = w_in.shape[0]
    past = cache_k.shape[2]
    yp = x_prompt.reshape(Bp * Lp, D)
    ys = x_sample.reshape(Bs * Ls, D)
    cond = jnp.concatenate([c_ctx[None, :], c, jnp.zeros((8 - 1 - Bs, D), F32)], axis=0)
    new_k, new_v = [], []
    row2 = lambda a: a.reshape(1, -1)
    col2 = lambda a: a.reshape(-1, 1)
    for l in range(depth):
        w1 = jnp.pad(filt_w1[l], ((0, FILTER_FEAT_PAD - FILTER_FEAT), (0, 0)))
        p = {
            'norm_mix_pre': row2(norm_mix_pre[l]), 'norm_mix_post': row2(norm_mix_post[l]),
            'norm_ffn_pre': row2(norm_ffn_pre[l]), 'norm_ffn_post': row2(norm_ffn_post[l]),
            'attn_sink': attn_sink[l], 'conv_w': conv_w[l], 'conv_b': conv_b[l],
            'filt_w1': w1.T, 'filt_b1': col2(filt_b1[l]), 'filt_freq1': col2(filt_freq1[l]),
            'filt_w2': filt_w2[l].T, 'filt_b2': col2(filt_b2[l]), 'filt_freq2': col2(filt_freq2[l]),
            'filt_w3': filt_w3[l], 'filt_b3': row2(filt_b3[l]), 'filt_decay': row2(filt_decay[l]),
            'hyena_skip': hyena_skip[l], 'b_gate': row2(b_gate[l]),
        }
        w = {'w_in': w_in[l], 'w_gate': w_gate[l], 'w_pa': w_pa[l], 'w_ph': w_ph[l], 'w_o': w_o[l],
             'w_up': w_up[l], 'w_down': w_down[l]}
        mod = _modulation(cond, w_mod[l], b_mod[l]).reshape(-1, 1, 6 * D)
        attn_p, hyo_p, k_ctx, v_ctx, w = _mixers(yp, mod[0:1], p, w, Bp, Lp, False, None, None)
        new_k.append(k_ctx.reshape(Bp, Lp, N_KV_HEADS, HEAD_DIM))
        new_v.append(v_ctx.reshape(Bp, Lp, N_KV_HEADS, HEAD_DIM))
        ck = cache_k[:, l].reshape(Bs * past, KV_WIDTH).astype(BF16)
        cv = jnp.swapaxes(cache_v[:, l].reshape(Bs, past, KV_WIDTH), 1, 2).reshape(Bs * KV_WIDTH, past).astype(BF16)
        attn_s, hyo_s, _, _, w = _mixers(ys, mod[1:1 + Bs], p, w, Bs, Ls, True, ck, cv)
        norms = [p['norm_mix_pre'], p['norm_mix_post'], p['norm_ffn_pre'], p['norm_ffn_post']]
        groups = [(yp, attn_p, hyo_p, 0, Bp * Lp), (ys, attn_s, hyo_s, 1, Ls)]
        yp, ys = _merge_ffn(groups, mod, norms, w['w_gate'], p['b_gate'], w['w_pa'], w['w_ph'], w['w_o'],
                            w['w_up'], w['w_down'])
    return (yp.reshape(Bp, Lp, D), ys.reshape(Bs, Ls, D),
            jnp.stack(new_k, axis=1), jnp.stack(new_v, axis=1))
```
